```python
import jax, jax.numpy as jnp
from jax import lax
import numpy as np

D_MODEL = 1024
BATCH = 4
SEQ = 8192
DEPTH = 1

N_HEADS = 16
HEAD_DIM = 64
ATTN_WIDTH = N_HEADS * HEAD_DIM
MOBA_BLOCK = 256
MOBA_TOPK = 3
MOBA_QBLOCK = 32
SSD_EXPAND = 2
SSD_INNER = SSD_EXPAND * D_MODEL
SSD_HEAD_DIM = 64
SSD_HEADS = SSD_INNER // SSD_HEAD_DIM
SSD_GROUPS = 8
SSD_STATE = 128
SSD_CONV = 4
SSD_CHUNK = 256
SSD_GN = SSD_GROUPS * SSD_STATE
SSD_CONV_CH = SSD_INNER + 2 * SSD_GN
PEER_HEADS = 8
PEER_NKEYS = 128
PEER_EXPERTS = PEER_NKEYS * PEER_NKEYS
PEER_QDIM = 256
PEER_TOPK = 16
PEER_TBLOCK = 128
N_BRANCH = 2
EPS = 1e-6
IN_COLS = 3 * ATTN_WIDTH + SSD_INNER + SSD_CONV_CH + SSD_HEADS + N_BRANCH * D_MODEL

kernel_name = "hybrid_moba_ssd_peer_block"


def rms_norm(x, g):
    x32 = x.astype(jnp.float32)
    y = x32 * lax.rsqrt(jnp.mean(x32 * x32, axis=-1, keepdims=True) + EPS)
    return (y * g.astype(jnp.float32)).astype(x.dtype)


def alibi_slopes(n):
    return jnp.exp2(-8.0 * jnp.arange(1, n + 1, dtype=jnp.float32) / n)


def moba_attention(q, k, v):
    B_, S_, H, hd = q.shape
    f32 = jnp.float32
    nblk = -(-S_ // MOBA_BLOCK)
    pad = nblk * MOBA_BLOCK - S_
    q = q.transpose(0, 2, 1, 3)
    padk = ((0, 0), (0, 0), (0, pad), (0, 0))
    kb = jnp.pad(k.transpose(0, 2, 1, 3), padk).reshape(B_, H, nblk, MOBA_BLOCK, hd)
    vb = jnp.pad(v.transpose(0, 2, 1, 3), padk).reshape(B_, H, nblk, MOBA_BLOCK, hd)
    slopes = alibi_slopes(H)
    kmean = jnp.mean(kb.astype(f32), axis=3).astype(q.dtype)
    t_all = jnp.arange(S_, dtype=jnp.int32)
    fully_past = jnp.arange(nblk, dtype=jnp.int32)[None, :] < (t_all // MOBA_BLOCK)[:, None]
    gate = jnp.einsum('bhsd,bhnd->bhsn', q, kmean).astype(f32)
    gate = jnp.where(fully_past, gate, -jnp.inf)
    k_sel = min(MOBA_TOPK, nblk)
    _, sel = lax.top_k(gate, k_sel)
    nq = S_ // MOBA_QBLOCK
    q_chunks = q.reshape(B_, H, nq, MOBA_QBLOCK, hd).transpose(2, 0, 1, 3, 4)
    sel_chunks = sel.reshape(B_, H, nq, MOBA_QBLOCK, k_sel).transpose(2, 0, 1, 3, 4)
    gather = jax.vmap(jax.vmap(lambda blocks, idx: blocks[idx]))
    scale = HEAD_DIM ** -0.5
    offs = jnp.arange(MOBA_BLOCK, dtype=jnp.int32)
    n_sel_keys = k_sel * MOBA_BLOCK

    def step(args):
        qc, sc, ci = args
        t = ci * MOBA_QBLOCK + jnp.arange(MOBA_QBLOCK, dtype=jnp.int32)
        kg = gather(kb, sc)
        vg = gather(vb, sc)
        s_sel = jnp.einsum('bhqd,bhqjld->bhqjl', qc, kg).astype(f32) * scale
        pos_sel = sc[..., None] * MOBA_BLOCK + offs
        dist_sel = (t[:, None, None] - pos_sel).astype(f32)
        valid = jnp.arange(k_sel, dtype=jnp.int32)[None, :] < (t // MOBA_BLOCK)[:, None]
        s_sel = jnp.where(valid[:, :, None], s_sel - slopes[:, None, None, None] * dist_sel, -jnp.inf)
        own = (ci * MOBA_QBLOCK) // MOBA_BLOCK
        k_own = lax.dynamic_index_in_dim(kb, own, axis=2, keepdims=False)
        v_own = lax.dynamic_index_in_dim(vb, own, axis=2, keepdims=False)
        dist_own = t[:, None] - (own * MOBA_BLOCK + offs)[None, :]
        s_own = jnp.einsum('bhqd,bhld->bhql', qc, k_own).astype(f32) * scale
        s_own = jnp.where(dist_own >= 0, s_own - slopes[:, None, None] * dist_own.astype(f32), -jnp.inf)
        s = jnp.concatenate([s_sel.reshape(B_, H, MOBA_QBLOCK, n_sel_keys), s_own], axis=-1)
        p = jax.nn.softmax(s, axis=-1).astype(vb.dtype)
        out = jnp.einsum('bhqm,bhqmd->bhqd', p[..., :n_sel_keys],
                         vg.reshape(B_, H, MOBA_QBLOCK, n_sel_keys, hd))
        out = out + jnp.einsum('bhql,bhld->bhqd', p[..., n_sel_keys:], v_own)
        return out

    outs = lax.map(step, (q_chunks, sel_chunks, jnp.arange(nq, dtype=jnp.int32)))
    return outs.transpose(1, 0, 3, 2, 4).reshape(B_, S_, H * hd)


def causal_depthwise_conv(u, w, b):
    out = lax.conv_general_dilated(u, w[:, None, :], window_strides=(1,),
                                   padding=[(SSD_CONV - 1, 0)],
                                   dimension_numbers=('NWC', 'WIO', 'NWC'),
                                   feature_group_count=u.shape[-1])
    return out + b


def ssd_mixer(z, xbc, dt_raw, conv_w, conv_b, dt_bias, a_log, d_skip, norm_g):
    out_dtype = z.dtype
    f32 = jnp.float32
    z, xbc, dt_raw = z.astype(f32), xbc.astype(f32), dt_raw.astype(f32)
    B_, S_, _ = xbc.shape
    G, R, P, N = SSD_GROUPS, SSD_HEADS // SSD_GROUPS, SSD_HEAD_DIM, SSD_STATE
    xbc = jax.nn.silu(causal_depthwise_conv(xbc, conv_w.astype(f32), conv_b.astype(f32)))
    xs = xbc[..., :SSD_INNER].reshape(B_, S_, G, R, P)
    bm = xbc[..., SSD_INNER:SSD_INNER + SSD_GN].reshape(B_, S_, G, N)
    cm = xbc[..., SSD_INNER + SSD_GN:].reshape(B_, S_, G, N)
    dt = jax.nn.softplus(dt_raw + dt_bias.astype(f32)).reshape(B_, S_, G, R)
    a = dt * (-jnp.exp(a_log.astype(f32))).reshape(G, R)
    nc = -(-S_ // SSD_CHUNK)
    pad = nc * SSD_CHUNK - S_

    def chunks(u):
        u = jnp.pad(u, [(0, 0), (0, pad)] + [(0, 0)] * (u.ndim - 2))
        return jnp.moveaxis(u.reshape((B_, nc, SSD_CHUNK) + u.shape[2:]), 1, 0)

    causal = jnp.tril(jnp.ones((SSD_CHUNK, SSD_CHUNK), dtype=bool))

    def step(state, inp):
        xc, ac, dtc, bc, cc = inp
        acum = jnp.cumsum(ac, axis=1)
        seg = acum[:, :, None] - acum[:, None, :]
        lmat = jnp.exp(jnp.where(causal[None, :, :, None, None], seg, -jnp.inf))
        xdt = xc * dtc[..., None]
        cb = jnp.einsum('btgn,bsgn->btsg', cc, bc)
        y_diag = jnp.einsum('btsg,btsgr,bsgrp->btgrp', cb, lmat, xdt)
        y_off = jnp.einsum('btgn,bgrpn->btgrp', cc, state) * jnp.exp(acum)[..., None]
        decay_end = jnp.exp(acum[:, -1:] - acum)
        new_state = state * jnp.exp(acum[:, -1])[..., None, None] + \
            jnp.einsum('bsgn,bsgr,bsgrp->bgrpn', bc, decay_end, xdt)
        return new_state, y_diag + y_off

    state0 = jnp.zeros((B_, G, R, P, N), f32)
    _, ys = lax.scan(step, state0, (chunks(xs), chunks(a), chunks(dt), chunks(bm), chunks(cm)))
    y = jnp.moveaxis(ys, 0, 1).reshape(B_, nc * SSD_CHUNK, G, R, P)[:, :S_]
    y = y + d_skip.astype(f32).reshape(G, R)[..., None] * xs
    y = y.reshape(B_, S_, SSD_INNER) * jax.nn.silu(z)
    yg = y.reshape(B_, S_, G, SSD_INNER // G)
    yg = yg * lax.rsqrt(jnp.mean(yg * yg, axis=-1, keepdims=True) + EPS)
    y = yg.reshape(B_, S_, SSD_INNER) * norm_g.astype(f32)
    return y.astype(out_dtype)


def peer_ffn(xn, w_q, keys1, keys2, u_tab, v_tab):
    B_, S_, D = xn.shape
    T = B_ * S_
    f32 = jnp.float32
    half = PEER_QDIM // 2
    xt = xn.reshape(T, D)
    q = (xt @ w_q).reshape(T, PEER_HEADS, PEER_QDIM)
    s1 = jnp.einsum('thd,hkd->thk', q[..., :half], keys1).astype(f32)
    s2 = jnp.einsum('thd,hkd->thk', q[..., half:], keys2).astype(f32)
    v1, i1 = lax.top_k(s1, PEER_TOPK)
    v2, i2 = lax.top_k(s2, PEER_TOPK)
    cand = (v1[..., :, None] + v2[..., None, :]).reshape(T, PEER_HEADS, PEER_TOPK * PEER_TOPK)
    cidx = (i1[..., :, None] * PEER_NKEYS + i2[..., None, :]).reshape(T, PEER_HEADS, PEER_TOPK * PEER_TOPK)
    sv, pos = lax.top_k(cand, PEER_TOPK)
    eidx = jnp.take_along_axis(cidx, pos, axis=-1)
    gw = jax.nn.softmax(sv, axis=-1)
    nt = T // PEER_TBLOCK

    def step(args):
        xc, ec, gc = args
        u = jnp.take(u_tab, ec, axis=0)
        act = jnp.einsum('td,thkd->thk', xc, u).astype(f32)
        hact = (jax.nn.gelu(act, approximate=False) * gc).astype(xc.dtype)
        return jnp.einsum('thk,thkd->td', hact, jnp.take(v_tab, ec, axis=0))

    y = lax.map(step, (xt.reshape(nt, PEER_TBLOCK, D),
                       eidx.reshape(nt, PEER_TBLOCK, PEER_HEADS, PEER_TOPK),
                       gw.reshape(nt, PEER_TBLOCK, PEER_HEADS, PEER_TOPK)))
    return y.reshape(B_, S_, D).astype(xn.dtype)


def setup_inputs(seed: int = 0) -> dict:
    key = jax.random.key(seed)
    ks = jax.random.split(key, 20)
    f32 = jnp.float32
    L = DEPTH

    def nrm(k, shape, scale):
        return jax.random.normal(k, shape, f32) * scale

    dt0 = jnp.exp(jax.random.uniform(ks[7], (L, SSD_HEADS), f32, np.log(1e-3), np.log(1e-1)))
    dt_bias = dt0 + jnp.log(-jnp.expm1(-dt0))
    return {
        "x": nrm(ks[0], (BATCH, SEQ, D_MODEL), 1.0),
        "norm1_g": 1.0 + nrm(ks[1], (L, D_MODEL), 0.02),
        "w_in": nrm(ks[2], (L, D_MODEL, IN_COLS), D_MODEL ** -0.5),
        "q_norm_g": 1.0 + nrm(ks[3], (L, HEAD_DIM), 0.02),
        "k_norm_g": 1.0 + nrm(ks[4], (L, HEAD_DIM), 0.02),
        "conv_w": nrm(ks[5], (L, SSD_CONV, SSD_CONV_CH), SSD_CONV ** -0.5),
        "conv_b": nrm(ks[6], (L, SSD_CONV_CH), 0.02),
        "dt_bias": dt_bias,
        "a_log": jnp.log(jax.random.uniform(ks[8], (L, SSD_HEADS), f32, 1.0, 16.0)),
        "d_skip": 1.0 + nrm(ks[9], (L, SSD_HEADS), 0.1),
        "ssd_norm_g": 1.0 + nrm(ks[10], (L, SSD_INNER), 0.02),
        "w_attn_o": nrm(ks[11], (L, ATTN_WIDTH, D_MODEL), ATTN_WIDTH ** -0.5),
        "w_ssd_o": nrm(ks[12], (L, SSD_INNER, D_MODEL), SSD_INNER ** -0.5),
        "w_out": nrm(ks[13], (L, D_MODEL, D_MODEL), D_MODEL ** -0.5),
        "norm2_g": 1.0 + nrm(ks[14], (L, D_MODEL), 0.02),
        "w_peer_q": nrm(ks[15], (L, D_MODEL, PEER_HEADS * PEER_QDIM), D_MODEL ** -0.5),
        "peer_keys1": nrm(ks[16], (L, PEER_HEADS, PEER_NKEYS, PEER_QDIM // 2), (PEER_QDIM // 2) ** -0.5),
        "peer_keys2": nrm(ks[17], (L, PEER_HEADS, PEER_NKEYS, PEER_QDIM // 2), (PEER_QDIM // 2) ** -0.5),
        "peer_u": nrm(ks[18], (L, PEER_EXPERTS, D_MODEL), D_MODEL ** -0.5),
        "peer_v": nrm(ks[19], (L, PEER_EXPERTS, D_MODEL), (PEER_HEADS * PEER_TOPK) ** -0.5),
    }


def reference(x, norm1_g, w_in, q_norm_g, k_norm_g, conv_w, conv_b, dt_bias, a_log, d_skip,
              ssd_norm_g, w_attn_o, w_ssd_o, w_out, norm2_g, w_peer_q, peer_keys1, peer_keys2,
              peer_u, peer_v):
    B_, S_, _ = x.shape
    sizes = [ATTN_WIDTH, ATTN_WIDTH, ATTN_WIDTH, SSD_INNER, SSD_CONV_CH, SSD_HEADS, D_MODEL, D_MODEL]
    split_at = [int(c) for c in np.cumsum(sizes)[:-1]]
    for l in range(DEPTH):
        h = rms_norm(x, norm1_g[l])
        proj = h @ w_in[l]
        q, k, v, z, xbc, dt_raw, g_attn, g_ssd = jnp.split(proj, split_at, axis=-1)
        q = rms_norm(q.reshape(B_, S_, N_HEADS, HEAD_DIM), q_norm_g[l])
        k = rms_norm(k.reshape(B_, S_, N_HEADS, HEAD_DIM), k_norm_g[l])
        v = v.reshape(B_, S_, N_HEADS, HEAD_DIM)
        y_attn = moba_attention(q, k, v) @ w_attn_o[l]
        y_ssd = ssd_mixer(z, xbc, dt_raw, conv_w[l], conv_b[l], dt_bias[l], a_log[l],
                          d_skip[l], ssd_norm_g[l]) @ w_ssd_o[l]
        mixed = jax.nn.sigmoid(g_attn) * y_attn + jax.nn.sigmoid(g_ssd) * y_ssd
        x = x + mixed @ w_out[l]
        x = x + peer_ffn(rms_norm(x, norm2_g[l]), w_peer_q[l], peer_keys1[l], peer_keys2[l],
                         peer_u[l], peer_v[l])
    return x
```

```python
import functools

import jax
import jax.numpy as jnp
import numpy as np
from jax import lax
from jax.experimental import pallas as pl
from jax.experimental.pallas import tpu as pltpu
from jax.experimental.pallas import tpu_sc as plsc

F32 = jnp.float32
BF16 = jnp.bfloat16
I32 = jnp.int32

EPS = 1e-6
D_MODEL = 1024
N_HEADS = 16
HEAD_DIM = 64
ATTN_WIDTH = N_HEADS * HEAD_DIM
MOBA_BLOCK = 256
MOBA_TOPK = 3
SSD_INNER = 2048
SSD_HEADS = 32
SSD_GROUPS = 8
SSD_STATE = 128
SSD_CONV = 4
SSD_CHUNK = 256
SSD_GN = SSD_GROUPS * SSD_STATE
SSD_GROUP_W = SSD_INNER // SSD_GROUPS
PEER_HEADS = 8
PEER_NKEYS = 128
PEER_QDIM = 256
PEER_TOPK = 16
PEER_SLOTS = PEER_HEADS * PEER_TOPK

LANES = 128
SUBLANES = 8
VMEM_LIMIT = 56 * 1024 * 1024
MASK_BIG = 1e30

COL_Q, COL_K, COL_V = 0, ATTN_WIDTH, 2 * ATTN_WIDTH
COL_Z = 3 * ATTN_WIDTH
COL_X = COL_Z + SSD_INNER
COL_B = COL_X + SSD_INNER
COL_C = COL_B + SSD_GN
COL_GA = COL_C + SSD_GN
COL_GS = COL_GA + D_MODEL
PROJ_COLS = COL_GS + D_MODEL

NT_DIMS = (((1,), (1,)), ((), ()))


def _cparams(sem):
    return pltpu.CompilerParams(dimension_semantics=sem, vmem_limit_bytes=VMEM_LIMIT)


def _norm_matmul_kernel(x_ref, g_ref, w_ref, o_ref, h_ref):
    @pl.when(pl.program_id(1) == 0)
    def _():
        x = x_ref[...]
        y = x * lax.rsqrt(jnp.mean(x * x, axis=-1, keepdims=True) + EPS)
        h_ref[...] = (y * g_ref[...]).astype(h_ref.dtype)

    o_ref[...] = jnp.dot(h_ref[...], w_ref[...], preferred_element_type=F32).astype(o_ref.dtype)


def norm_matmul(x, g, w, out_dtype, tm, tn):
    T, K = x.shape
    N = w.shape[1]
    return pl.pallas_call(
        _norm_matmul_kernel,
        grid=(T // tm, N // tn),
        in_specs=[
            pl.BlockSpec((tm, K), lambda i, j: (i, 0)),
            pl.BlockSpec((1, K), lambda i, j: (0, 0)),
            pl.BlockSpec((K, tn), lambda i, j: (0, j)),
        ],
        out_specs=pl.BlockSpec((tm, tn), lambda i, j: (i, j)),
        out_shape=jax.ShapeDtypeStruct((T, N), out_dtype),
        scratch_shapes=[pltpu.VMEM((tm, K), BF16)],
        compiler_params=_cparams(("parallel", "arbitrary")),
        name="norm_matmul",
    )(x, g, w)


def _split3(v):
    hi = v.astype(BF16).astype(F32)
    r1 = v - hi
    mid = r1.astype(BF16).astype(F32)
    return hi, mid, r1 - mid


def _head_pair_norm(x, g):
    lane = lax.broadcasted_iota(I32, x.shape, 1)
    low = lane < HEAD_DIM
    x2 = x * x
    ss_a = jnp.sum(jnp.where(low, x2, 0.0), axis=-1, keepdims=True)
    ss_b = jnp.sum(jnp.where(low, 0.0, x2), axis=-1, keepdims=True)
    inv = jnp.where(low, lax.rsqrt(ss_a / HEAD_DIM + EPS), lax.rsqrt(ss_b / HEAD_DIM + EPS))
    return x * inv * g


def _kprep_kernel(k_ref, g_ref, sl_ref, kaug_ref, kmean_ref, *, tq):
    s_idx = pl.program_id(2)
    kn = _head_pair_norm(k_ref[...].astype(F32), g_ref[...])
    nb = tq // MOBA_BLOCK
    km = jnp.mean(kn.reshape(nb, MOBA_BLOCK, LANES), axis=1)
    lane = lax.broadcasted_iota(I32, (tq, LANES), 1)
    row = lax.broadcasted_iota(I32, (tq, LANES), 0) + s_idx * tq
    blk = row // MOBA_BLOCK
    pos = row.astype(F32)
    lane_m = lax.broadcasted_iota(I32, (nb, LANES), 1)
    heads = ((kn, km), (pltpu.roll(kn, HEAD_DIM, axis=1), pltpu.roll(km, HEAD_DIM, axis=1)))
    for hh, (kk, kmm) in enumerate(heads):
        hi, mid, lo = _split3(sl_ref[hh:hh + 1, :] * pos)
        aug = jnp.where(lane < HEAD_DIM, kk, 0.0)
        aug = jnp.where((lane >= 64) & (lane < 96), (lane - 64 == blk).astype(F32), aug)
        aug = jnp.where(lane == 96, hi, aug)
        aug = jnp.where(lane == 97, mid, aug)
        aug = jnp.where(lane == 98, lo, aug)
        aug = jnp.where((lane >= 99) & (lane < 102), 1.0, aug)
        kaug_ref[hh] = aug.astype(BF16)
        kmean_ref[hh] = jnp.where(lane_m < HEAD_DIM, kmm, 0.0)


def _qprep_kernel(q_ref, g_ref, sl_ref, kmean_ref, qaug_ref, *, tq, nblk):
    s_idx = pl.program_id(2)
    qn = _head_pair_norm(q_ref[...].astype(F32), g_ref[...])
    lane = lax.broadcasted_iota(I32, (tq, LANES), 1)
    row = lax.broadcasted_iota(I32, (tq, LANES), 0) + s_idx * tq
    own = row // MOBA_BLOCK
    t = row.astype(F32)
    jl = lane - 64
    heads = (qn, pltpu.roll(qn, HEAD_DIM, axis=1))
    for hh, qh in enumerate(heads):
        qq = jnp.where(lane < HEAD_DIM, qh, 0.0)
        km_rows = jnp.concatenate(
            [jnp.zeros((64, LANES), F32), kmean_ref[hh], jnp.zeros((64 - nblk, LANES), F32)], axis=0)
        gate = lax.dot_general(qq, km_rows, NT_DIMS, precision=lax.Precision.HIGHEST,
                               preferred_element_type=F32)
        g = jnp.where((jl >= 0) & (jl < own), gate, -jnp.inf)
        allowed = jl == own
        for r in range(MOBA_TOPK):
            m = jnp.max(g, axis=-1, keepdims=True)
            first = jnp.min(jnp.where(g == m, lane, 1 << 20), axis=-1, keepdims=True)
            hit = lane == first
            allowed = allowed | (hit & (own > r))
            g = jnp.where(hit, -jnp.inf, g)
        hi, mid, lo = _split3(-sl_ref[hh:hh + 1, :] * t)
        aug = qq * (HEAD_DIM ** -0.5)
        aug = jnp.where((jl >= 0) & (jl < 32), jnp.where(allowed, 0.0, -MASK_BIG), aug)
        aug = jnp.where((lane >= 96) & (lane < 99), 1.0, aug)
        aug = jnp.where(lane == 99, hi, aug)
        aug = jnp.where(lane == 100, mid, aug)
        aug = jnp.where(lane == 101, lo, aug)
        qaug_ref[hh] = aug.astype(BF16)


def _attn_kernel(q_ref, k_ref, v_ref, o_ref, acc_ref, m_ref, *, tq):
    i = pl.program_id(2)
    lane = lax.broadcasted_iota(I32, (tq, LANES), 1)
    low = lane < HEAD_DIM
    rr = lax.broadcasted_iota(I32, (tq, tq), 0)
    cc = lax.broadcasted_iota(I32, (tq, tq), 1)
    causal = cc <= rr

    def tile(j, diag):
        start = pl.multiple_of(j * tq, tq)
        vj = v_ref[pl.ds(start, tq), :]
        one = jnp.ones_like(vj)
        for hh in range(2):
            kj = k_ref[hh, pl.ds(start, tq), :]
            s = lax.dot_general(q_ref[hh], kj, NT_DIMS, preferred_element_type=F32)
            if diag:
                s = jnp.where(causal, s, -MASK_BIG)
            m_old = m_ref[hh]
            m_new = jnp.maximum(m_old, jnp.max(s, axis=-1, keepdims=True))
            alpha = jnp.exp(m_old - m_new)
            p = jnp.exp(s - m_new).astype(BF16)
            vaug = jnp.where(low, vj, one) if hh == 0 else jnp.where(low, one, vj)
            acc_ref[hh] = alpha * acc_ref[hh] + jnp.dot(p, vaug, preferred_element_type=F32)
            m_ref[hh] = m_new

    acc_ref[...] = jnp.zeros_like(acc_ref)
    m_ref[...] = jnp.full_like(m_ref, -MASK_BIG)
    tile(i, True)

    def body(j, carry):
        tile(j, False)
        return carry

    lax.fori_loop(0, i, body, 0)
    a = acc_ref[0]
    b = acc_ref[1]
    out = jnp.where(low, a / pltpu.roll(a, HEAD_DIM, axis=1), b / pltpu.roll(b, HEAD_DIM, axis=1))
    o_ref[...] = out.astype(o_ref.dtype)


def moba_attention(proj3, q_norm_g, k_norm_g):
    B, S, _ = proj3.shape
    nblk = S // MOBA_BLOCK
    assert S % MOBA_BLOCK == 0 and nblk <= 32
    HP = N_HEADS // 2
    tq = min(2048, S)
    slopes = jnp.exp2(-8.0 * jnp.arange(1, N_HEADS + 1, dtype=F32) / N_HEADS)
    sl = jnp.zeros((HP, SUBLANES, LANES), F32)
    sl = sl.at[:, 0, :].set(slopes[0::2, None]).at[:, 1, :].set(slopes[1::2, None])
    gq = jnp.tile(q_norm_g.astype(F32), 2)[None, :]
    gk = jnp.tile(k_norm_g.astype(F32), 2)[None, :]
    qb, kb, vb = COL_Q // LANES, COL_K // LANES, COL_V // LANES
    grid = (B, HP, S // tq)
    sem3 = ("parallel", "parallel", "parallel")

    kaug, kmean = pl.pallas_call(
        functools.partial(_kprep_kernel, tq=tq),
        grid=grid,
        in_specs=[
            pl.BlockSpec((None, tq, LANES), lambda b, p, s: (b, s, kb + p)),
            pl.BlockSpec((1, LANES), lambda b, p, s: (0, 0)),
            pl.BlockSpec((None, SUBLANES, LANES), lambda b, p, s: (p, 0, 0)),
        ],
        out_specs=[
            pl.BlockSpec((None, 2, tq, LANES), lambda b, p, s: (b, p, s, 0)),
            pl.BlockSpec((None, 2, tq // MOBA_BLOCK, LANES), lambda b, p, s: (b, p, s, 0)),
        ],
        out_shape=[
            jax.ShapeDtypeStruct((B, N_HEADS, S, LANES), BF16),
            jax.ShapeDtypeStruct((B, N_HEADS, nblk, LANES), F32),
        ],
        compiler_params=_cparams(sem3),
        name="moba_kprep",
    )(proj3, gk, sl)

    qaug = pl.pallas_call(
        functools.partial(_qprep_kernel, tq=tq, nblk=nblk),
        grid=grid,
        in_specs=[
            pl.BlockSpec((None, tq, LANES), lambda b, p, s: (b, s, qb + p)),
            pl.BlockSpec((1, LANES), lambda b, p, s: (0, 0)),
            pl.BlockSpec((None, SUBLANES, LANES), lambda b, p, s: (p, 0, 0)),
            pl.BlockSpec((None, 2, nblk, LANES), lambda b, p, s: (b, p, 0, 0)),
        ],
        out_specs=pl.BlockSpec((None, 2, tq, LANES), lambda b, p, s: (b, p, s, 0)),
        out_shape=jax.ShapeDtypeStruct((B, N_HEADS, S, LANES), BF16),
        compiler_params=_cparams(sem3),
        name="moba_qprep",
    )(proj3, gq, sl, kmean)

    ta = MOBA_BLOCK
    return pl.pallas_call(
        functools.partial(_attn_kernel, tq=ta),
        grid=(B, HP, S // ta),
        in_specs=[
            pl.BlockSpec((None, 2, ta, LANES), lambda b, p, i: (b, p, i, 0)),
            pl.BlockSpec((None, 2, S, LANES), lambda b, p, i: (b, p, 0, 0)),
            pl.BlockSpec((None, S, LANES), lambda b, p, i: (b, 0, vb + p)),
        ],
        out_specs=pl.BlockSpec((None, ta, LANES), lambda b, p, i: (b, i, p)),
        out_shape=jax.ShapeDtypeStruct((B, S, ATTN_WIDTH), BF16),
        scratch_shapes=[pltpu.VMEM((2, ta, LANES), F32), pltpu.VMEM((2, ta, 1), F32)],
        compiler_params=_cparams(("parallel", "parallel", "arbitrary")),
        name="moba_attn",
    )(qaug, kaug, proj3)


def _silu(x):
    return x * (1.0 / (1.0 + jnp.exp(-x)))


def _conv_silu(u_ref, halo_ref, w_ref, b_ref, first):
    u = u_ref[...].astype(F32)
    halo = jnp.where(first, 0.0, halo_ref[...].astype(F32))
    ext = jnp.concatenate([halo, u], axis=0)
    w = w_ref[...]
    out = b_ref[...] + w[3:4, :] * u
    for back in range(1, SSD_CONV):
        out = out + w[3 - back:4 - back, :] * pltpu.roll(ext, back, axis=0)[SUBLANES:, :]
    return _silu(out)


def _ssd_kernel(x_ref, xh_ref, b_ref, bh_ref, c_ref, ch_ref, z_ref, dt_ref,
                wx_ref, wb_ref, wc_ref, bx_ref, bb_ref, bc_ref, dtb_ref, rg_ref,
                a_ref, d_ref, ng_ref, y_ref, state_ref):
    c = pl.program_id(2)
    L = SSD_CHUNK
    W = SSD_GROUP_W
    first = c == 0

    @pl.when(first)
    def _():
        state_ref[...] = jnp.zeros_like(state_ref)

    xs = _conv_silu(x_ref, xh_ref, wx_ref, bx_ref, first)
    bm = _conv_silu(b_ref, bh_ref, wb_ref, bb_ref, first)
    cm = _conv_silu(c_ref, ch_ref, wc_ref, bc_ref, first)
    dt = jax.nn.softplus(dt_ref[...] + dtb_ref[...])
    hp = lax.Precision.HIGHEST
    dtx = jnp.dot(dt, rg_ref[...], precision=hp, preferred_element_type=F32)
    ax = dtx * a_ref[...]
    rr = lax.broadcasted_iota(I32, (L, L), 0)
    cc = lax.broadcasted_iota(I32, (L, L), 1)
    causal = cc <= rr
    acum = jnp.dot(causal.astype(F32), ax, precision=hp, preferred_element_type=F32)
    acum_t = acum.T
    a_last = acum[L - 1:L, :]
    xdt = xs * dtx
    cmb = cm.astype(BF16)
    cb = lax.dot_general(cmb, bm.astype(BF16), NT_DIMS, preferred_element_type=F32)
    lane = lax.broadcasted_iota(I32, (L, W), 1)
    y = jnp.zeros((L, W), F32)
    for r in range(W // 64):
        seg = acum[:, 64 * r:64 * r + 1] - acum_t[64 * r:64 * r + 1, :]
        lmat = jnp.exp(jnp.where(causal, seg, -jnp.inf))
        xr = jnp.where((lane >= 64 * r) & (lane < 64 * r + 64), xdt, 0.0).astype(BF16)
        y = y + jnp.dot((cb * lmat).astype(BF16), xr, preferred_element_type=F32)
    state = state_ref[...]
    y = y + jnp.dot(cmb, state.astype(BF16), preferred_element_type=F32) * jnp.exp(acum)
    wgt = (xdt * jnp.exp(a_last - acum)).astype(BF16)
    state_ref[...] = state * jnp.exp(a_last) + jnp.dot(bm.T.astype(BF16), wgt, preferred_element_type=F32)
    y = y + d_ref[...] * xs
    y = y * _silu(z_ref[...].astype(F32))
    y = y * lax.rsqrt(jnp.mean(y * y, axis=-1, keepdims=True) + EPS)
    y_ref[...] = (y * ng_ref[...]).astype(y_ref.dtype)


def ssd_mixer(proj3, dtraw3, conv_w, conv_b, dt_bias, a_log, d_skip, norm_g):
    B, S, _ = proj3.shape
    L, W, N, G = SSD_CHUNK, SSD_GROUP_W, SSD_STATE, SSD_GROUPS
    assert S % L == 0
    nc = S // L
    hb = L // SUBLANES
    xb, bb, cb_, zb = COL_X // W, COL_B // N, COL_C // N, COL_Z // W
    rep = SSD_INNER // SSD_HEADS
    a_exp = jnp.repeat(-jnp.exp(a_log.astype(F32)), rep)[None, :]
    d_exp = jnp.repeat(d_skip.astype(F32), rep)[None, :]
    ng = norm_g.astype(F32)[None, :]
    dtb = jnp.zeros((1, LANES), F32).at[0, :SSD_HEADS].set(dt_bias.astype(F32))
    head_of_chan = np.arange(SSD_INNER) // rep
    rg = (np.arange(LANES)[None, :, None] == head_of_chan.reshape(G, 1, W)).astype(np.float32)
    cw = conv_w.astype(F32)
    cbias = conv_b.astype(F32)[None, :]
    cxo, cbo, cco = 0, SSD_INNER // N, (SSD_INNER + SSD_GN) // N

    def halo(col):
        return lambda b, g, c: (b, jnp.maximum(c * hb - 1, 0), col + g)

    return pl.pallas_call(
        _ssd_kernel,
        grid=(B, G, nc),
        in_specs=[
            pl.BlockSpec((None, L, W), lambda b, g, c: (b, c, xb + g)),
            pl.BlockSpec((None, SUBLANES, W), halo(xb)),
            pl.BlockSpec((None, L, N), lambda b, g, c: (b, c, bb + g)),
            pl.BlockSpec((None, SUBLANES, N), halo(bb)),
            pl.BlockSpec((None, L, N), lambda b, g, c: (b, c, cb_ + g)),
            pl.BlockSpec((None, SUBLANES, N), halo(cb_)),
            pl.BlockSpec((None, L, W), lambda b, g, c: (b, c, zb + g)),
            pl.BlockSpec((None, L, LANES), lambda b, g, c: (b, c, 0)),
            pl.BlockSpec((SSD_CONV, W), lambda b, g, c: (0, cxo + g)),
            pl.BlockSpec((SSD_CONV, N), lambda b, g, c: (0, cbo + g)),
            pl.BlockSpec((SSD_CONV, N), lambda b, g, c: (0, cco + g)),
            pl.BlockSpec((1, W), lambda b, g, c: (0, cxo + g)),
            pl.BlockSpec((1, N), lambda b, g, c: (0, cbo + g)),
            pl.BlockSpec((1, N), lambda b, g, c: (0, cco + g)),
            pl.BlockSpec((1, LANES), lambda b, g, c: (0, 0)),
            pl.BlockSpec((None, LANES, W), lambda b, g, c: (g, 0, 0)),
            pl.BlockSpec((1, W), lambda b, g, c: (0, g)),
            pl.BlockSpec((1, W), lambda b, g, c: (0, g)),
            pl.BlockSpec((1, W), lambda b, g, c: (0, g)),
        ],
        out_specs=pl.BlockSpec((None, L, W), lambda b, g, c: (b, c, g)),
        out_shape=jax.ShapeDtypeStruct((B, S, SSD_INNER), BF16),
        scratch_shapes=[pltpu.VMEM((N, W), F32)],
        compiler_params=_cparams(("parallel", "parallel", "arbitrary")),
        name="ssd_scan",
    )(proj3, proj3, proj3, proj3, proj3, proj3, proj3, dtraw3,
      cw, cw, cw, cbias, cbias, cbias, dtb, jnp.asarray(rg), a_exp, d_exp, ng)


def _merge_kernel(x_ref, a_ref, s_ref, ga_ref, gs_ref, wa_ref, ws_ref, wo_ref, o_ref):
    ya = jnp.dot(a_ref[...], wa_ref[...], preferred_element_type=F32)
    ys = jnp.dot(s_ref[...], ws_ref[...], preferred_element_type=F32)
    mixed = jax.nn.sigmoid(ga_ref[...].astype(F32)) * ya + jax.nn.sigmoid(gs_ref[...].astype(F32)) * ys
    o_ref[...] = x_ref[...] + jnp.dot(mixed.astype(BF16), wo_ref[...], preferred_element_type=F32)


def merge_branches(x2, attn2, ssd2, proj2, wa, ws, wo, tm):
    T, D = x2.shape
    full = lambda a: pl.BlockSpec(a.shape, lambda i: (0, 0))
    return pl.pallas_call(
        _merge_kernel,
        grid=(T // tm,),
        in_specs=[
            pl.BlockSpec((tm, D), lambda i: (i, 0)),
            pl.BlockSpec((tm, ATTN_WIDTH), lambda i: (i, 0)),
            pl.BlockSpec((tm, SSD_INNER), lambda i: (i, 0)),
            pl.BlockSpec((tm, D), lambda i: (i, COL_GA // D)),
            pl.BlockSpec((tm, D), lambda i: (i, COL_GS // D)),
            full(wa), full(ws), full(wo),
        ],
        out_specs=pl.BlockSpec((tm, D), lambda i: (i, 0)),
        out_shape=jax.ShapeDtypeStruct((T, D), F32),
        compiler_params=_cparams(("parallel",)),
        name="merge",
    )(x2, attn2, ssd2, proj2, proj2, wa, ws, wo)


def _topk_rows(s, idx, k):
    n = s.shape[0]
    row = lax.broadcasted_iota(I32, s.shape, 0)
    vals, rows, picked = [], [], []
    for _ in range(k):
        m = jnp.max(s, axis=0, keepdims=True)
        first = jnp.min(jnp.where(s == m, row, n), axis=0, keepdims=True)
        hit = row == first
        vals.append(m)
        rows.append(first)
        if idx is not None:
            picked.append(jnp.max(jnp.where(hit, idx, -1), axis=0, keepdims=True))
        s = jnp.where(hit, -jnp.inf, s)
    cat = lambda xs: jnp.concatenate(xs, axis=0)
    return cat(vals), cat(rows), (cat(picked) if idx is not None else None)


def _peer_topk_kernel(q_ref, k1_ref, k2_ref, eidx_ref, gw_ref, *, tt):
    half = PEER_QDIM // 2
    e_rows, g_rows = [], []
    for h in range(PEER_HEADS):
        qa = q_ref[:, h * PEER_QDIM:h * PEER_QDIM + half].astype(BF16)
        qb = q_ref[:, h * PEER_QDIM + half:(h + 1) * PEER_QDIM].astype(BF16)
        s1 = lax.dot_general(k1_ref[h], qa, NT_DIMS, preferred_element_type=F32)
        s2 = lax.dot_general(k2_ref[h], qb, NT_DIMS, preferred_element_type=F32)
        v1, i1, _ = _topk_rows(s1, None, PEER_TOPK)
        v2, i2, _ = _topk_rows(s2, None, PEER_TOPK)
        cand = jnp.concatenate([v1[a:a + 1, :] + v2 for a in range(PEER_TOPK)], axis=0)
        cidx = jnp.concatenate([i1[a:a + 1, :] * PEER_NKEYS + i2 for a in range(PEER_TOPK)], axis=0)
        sv, _, ex = _topk_rows(cand, cidx, PEER_TOPK)
        e = jnp.exp(sv - sv[0:1, :])
        g_rows.append(e / jnp.sum(e, axis=0, keepdims=True))
        e_rows.append(ex)
    eidx_ref[...] = jnp.concatenate(e_rows, axis=0).T
    gw = jnp.concatenate(g_rows, axis=0)
    for part in range(tt // LANES):
        gw_ref[part] = gw[:, part * LANES:(part + 1) * LANES]


def peer_topk(q, keys1, keys2, tt):
    T = q.shape[0]
    full3 = lambda a: pl.BlockSpec(a.shape, lambda i: (0, 0, 0))
    return pl.pallas_call(
        functools.partial(_peer_topk_kernel, tt=tt),
        grid=(T // tt,),
        in_specs=[pl.BlockSpec((tt, q.shape[1]), lambda i: (i, 0)), full3(keys1), full3(keys2)],
        out_specs=[
            pl.BlockSpec((tt, PEER_SLOTS), lambda i: (i, 0)),
            pl.BlockSpec((tt // LANES, PEER_SLOTS, LANES), lambda i: (i, 0, 0)),
        ],
        out_shape=[
            jax.ShapeDtypeStruct((T, PEER_SLOTS), I32),
            jax.ShapeDtypeStruct((T // LANES, PEER_SLOTS, LANES), F32),
        ],
        compiler_params=_cparams(("parallel",)),
        name="peer_topk",
    )(q, keys1, keys2)


def pack_table(tab):
    half = tab.shape[1] // 2
    bits = lax.bitcast_convert_type(tab.astype(BF16), jnp.uint16).astype(jnp.uint32)
    return lax.bitcast_convert_type((bits[:, :half] << 16) | bits[:, half:], I32)


def sc_gather_rows(table, idx, win=64):
    V, W = table.shape
    N = idx.shape[0]
    info = plsc.get_sparse_core_info()
    n_cores, n_sub = info.num_cores, info.num_subcores
    workers = n_cores * n_sub
    per_w = N // workers
    assert per_w * workers == N and per_w % win == 0
    steps = per_w // win
    mesh = plsc.VectorSubcoreMesh(core_axis_name="c", subcore_axis_name="s")

    @functools.partial(
        pl.kernel, mesh=mesh,
        out_type=jax.ShapeDtypeStruct((N, W), table.dtype),
        scratch_types=[
            pltpu.VMEM((win,), I32),
            pltpu.VMEM((win, W), table.dtype),
            pltpu.SemaphoreType.DMA,
        ],
    )
    def gather_kernel(table_hbm, idx_hbm, out_hbm, idx_v, rows_v, sem):
        wid = lax.axis_index("s") * n_cores + lax.axis_index("c")
        base = wid * per_w

        @pl.loop(0, steps)
        def _(i):
            off = pl.multiple_of(base + i * win, SUBLANES)
            pltpu.sync_copy(idx_hbm.at[pl.ds(off, win)], idx_v)
            pltpu.async_copy(table_hbm.at[idx_v], rows_v, sem).wait()
            pltpu.sync_copy(rows_v, out_hbm.at[pl.ds(off, win)])

    return gather_kernel(table, idx)


def _unpack_words(w):
    u = pltpu.bitcast(w, jnp.uint32)
    hi = pltpu.bitcast(u & jnp.uint32(0xFFFF0000), F32)
    lo = pltpu.bitcast(u << 16, F32)
    return hi, lo


def _peer_expert_kernel(x_ref, g_ref, ug_ref, vg_ref, gw_ref, o_ref, *, tt):
    i = pl.program_id(0)
    half = D_MODEL // 2
    x1 = x_ref[...]
    xn = x1 * lax.rsqrt(jnp.mean(x1 * x1, axis=-1, keepdims=True) + EPS) * g_ref[...]
    lane = lax.broadcasted_iota(I32, (PEER_SLOTS, LANES), 1)
    off = (i % (LANES // tt)) * tt
    act = jnp.zeros((PEER_SLOTS, LANES), F32)
    for t in range(tt):
        hi, lo = _unpack_words(ug_ref[t * PEER_SLOTS:(t + 1) * PEER_SLOTS, :])
        prod = hi * xn[t:t + 1, :half] + lo * xn[t:t + 1, half:]
        fold = prod[:, 0:LANES]
        for c in range(1, half // LANES):
            fold = fold + prod[:, c * LANES:(c + 1) * LANES]
        col = jnp.sum(fold, axis=-1, keepdims=True)
        act = jnp.where(lane == off + t, col, act)
    gelu = 0.5 * act * (1.0 + lax.erf(act * (2.0 ** -0.5)))
    hact = gelu * gw_ref[...]
    for t in range(tt):
        hcol = jnp.sum(jnp.where(lane == off + t, hact, 0.0), axis=-1, keepdims=True)
        hi, lo = _unpack_words(vg_ref[t * PEER_SLOTS:(t + 1) * PEER_SLOTS, :])
        o_ref[t:t + 1, :half] = x1[t:t + 1, :half] + jnp.sum(hcol * hi, axis=0, keepdims=True)
        o_ref[t:t + 1, half:] = x1[t:t + 1, half:] + jnp.sum(hcol * lo, axis=0, keepdims=True)


def peer_experts(x1, g2, ug, vg, gw, tt=16):
    T, D = x1.shape
    W = ug.shape[1]
    per = LANES // tt
    return pl.pallas_call(
        functools.partial(_peer_expert_kernel, tt=tt),
        grid=(T // tt,),
        in_specs=[
            pl.BlockSpec((tt, D), lambda i: (i, 0)),
            pl.BlockSpec((1, D), lambda i: (0, 0)),
            pl.BlockSpec((tt * PEER_SLOTS, W), lambda i: (i, 0)),
            pl.BlockSpec((tt * PEER_SLOTS, W), lambda i: (i, 0)),
            pl.BlockSpec((None, PEER_SLOTS, LANES), lambda i: (i // per, 0, 0)),
        ],
        out_specs=pl.BlockSpec((tt, D), lambda i: (i, 0)),
        out_shape=jax.ShapeDtypeStruct((T, D), F32),
        compiler_params=_cparams(("parallel",)),
        name="peer_experts",
    )(x1, g2, ug, vg, gw)


def peer_ffn_residual(x1, norm2_g, w_q, keys1, keys2, u_tab, v_tab, n_chunks):
    T = x1.shape[0]
    g2 = norm2_g.astype(F32)[None, :]
    q = norm_matmul(x1, g2, w_q.astype(BF16), F32, min(1024, T), 512)
    eidx, gw = peer_topk(q, keys1.astype(BF16), keys2.astype(BF16), min(256, T))
    up, vp = pack_table(u_tab), pack_table(v_tab)
    tc = T // n_chunks
    outs = []
    for c in range(n_chunks):
        flat = eidx[c * tc:(c + 1) * tc].reshape(-1)
        ug = sc_gather_rows(up, flat)
        vg = sc_gather_rows(vp, flat)
        outs.append(peer_experts(x1[c * tc:(c + 1) * tc], g2, ug, vg,
                                 gw[c * tc // LANES:(c + 1) * tc // LANES]))
    return jnp.concatenate(outs, axis=0) if n_chunks > 1 else outs[0]


def kernel(x, norm1_g, w_in, q_norm_g, k_norm_g, conv_w, conv_b, dt_bias, a_log, d_skip, ssd_norm_g,
           w_attn_o, w_ssd_o, w_out, norm2_g, w_peer_q, peer_keys1, peer_keys2, peer_u, peer_v):
    B, S, D = x.shape
    T = B * S
    x2 = x.reshape(T, D)
    for l in range(norm1_g.shape[0]):
        w = w_in[l]
        dt0 = COL_GA
        w_main = jnp.concatenate([w[:, :dt0], w[:, dt0 + SSD_HEADS:]], axis=1).astype(BF16)
        w_dt = jnp.zeros((D, LANES), BF16).at[:, :SSD_HEADS].set(w[:, dt0:dt0 + SSD_HEADS].astype(BF16))
        g1 = norm1_g[l].astype(F32)[None, :]
        tm = min(1024, T)
        proj = norm_matmul(x2, g1, w_main, BF16, tm, 512)
        dtraw = norm_matmul(x2, g1, w_dt, F32, tm, LANES)
        proj3 = proj.reshape(B, S, PROJ_COLS)
        attn = moba_attention(proj3, q_norm_g[l], k_norm_g[l])
        yssd = ssd_mixer(proj3, dtraw.reshape(B, S, LANES), conv_w[l], conv_b[l], dt_bias[l],
                         a_log[l], d_skip[l], ssd_norm_g[l])
        x1 = merge_branches(x2, attn.reshape(T, ATTN_WIDTH), yssd.reshape(T, SSD_INNER), proj,
                            w_attn_o[l].astype(BF16), w_ssd_o[l].astype(BF16), w_out[l].astype(BF16),
                            min(512, T))
        x2 = peer_ffn_residual(x1, norm2_g[l], w_peer_q[l], peer_keys1[l], peer_keys2[l],
                               peer_u[l], peer_v[l], max(1, T // 8192))
    return x2.reshape(B, S, D)
```

```python
import functools

import jax
import jax.numpy as jnp
import numpy as np
from jax import lax
from jax.experimental import pallas as pl
from jax.experimental.pallas import tpu as pltpu
from jax.experimental.pallas import tpu_sc as plsc

F32 = jnp.float32
BF16 = jnp.bfloat16
I32 = jnp.int32

EPS = 1e-6
D_MODEL = 1024
N_HEADS = 16
HEAD_DIM = 64
ATTN_WIDTH = N_HEADS * HEAD_DIM
MOBA_BLOCK = 256
MOBA_TOPK = 3
MOBA_KEY_GROUP = 4
SSD_INNER = 2048
SSD_HEADS = 32
SSD_GROUPS = 8
SSD_STATE = 128
SSD_CONV = 4
SSD_CHUNK = 256
SSD_GN = SSD_GROUPS * SSD_STATE
SSD_GROUP_W = SSD_INNER // SSD_GROUPS
PEER_HEADS = 8
PEER_NKEYS = 128
PEER_QDIM = 256
PEER_TOPK = 16
PEER_SLOTS = PEER_HEADS * PEER_TOPK

LANES = 128
SUBLANES = 8
VMEM_LIMIT = 56 * 1024 * 1024
MASK_BIG = 1e30

COL_Q, COL_K, COL_V = 0, ATTN_WIDTH, 2 * ATTN_WIDTH
COL_Z = 3 * ATTN_WIDTH
COL_X = COL_Z + SSD_INNER
COL_B = COL_X + SSD_INNER
COL_C = COL_B + SSD_GN
COL_GA = COL_C + SSD_GN
COL_GS = COL_GA + D_MODEL
PROJ_COLS = COL_GS + D_MODEL

NT_DIMS = (((1,), (1,)), ((), ()))


def _cparams(sem):
    return pltpu.CompilerParams(dimension_semantics=sem, vmem_limit_bytes=VMEM_LIMIT)


def _norm_matmul_kernel(x_ref, g_ref, w_ref, o_ref, h_ref):
    @pl.when(pl.program_id(1) == 0)
    def _():
        x = x_ref[...]
        y = x * lax.rsqrt(jnp.mean(x * x, axis=-1, keepdims=True) + EPS)
        h_ref[...] = (y * g_ref[...]).astype(h_ref.dtype)

    o_ref[...] = jnp.dot(h_ref[...], w_ref[...], preferred_element_type=F32).astype(o_ref.dtype)


def norm_matmul(x, g, w, out_dtype, tm, tn):
    T, K = x.shape
    N = w.shape[1]
    return pl.pallas_call(
        _norm_matmul_kernel,
        grid=(T // tm, N // tn),
        in_specs=[
            pl.BlockSpec((tm, K), lambda i, j: (i, 0)),
            pl.BlockSpec((1, K), lambda i, j: (0, 0)),
            pl.BlockSpec((K, tn), lambda i, j: (0, j)),
        ],
        out_specs=pl.BlockSpec((tm, tn), lambda i, j: (i, j)),
        out_shape=jax.ShapeDtypeStruct((T, N), out_dtype),
        scratch_shapes=[pltpu.VMEM((tm, K), BF16)],
        compiler_params=_cparams(("parallel", "arbitrary")),
        name="norm_matmul",
    )(x, g, w)


def _split3(v):
    hi = v.astype(BF16).astype(F32)
    r1 = v - hi
    mid = r1.astype(BF16).astype(F32)
    return hi, mid, r1 - mid


def _head_pair_norm(x, g):
    lane = lax.broadcasted_iota(I32, x.shape, 1)
    low = lane < HEAD_DIM
    x2 = x * x
    ss_a = jnp.sum(jnp.where(low, x2, 0.0), axis=-1, keepdims=True)
    ss_b = jnp.sum(jnp.where(low, 0.0, x2), axis=-1, keepdims=True)
    inv = jnp.where(low, lax.rsqrt(ss_a / HEAD_DIM + EPS), lax.rsqrt(ss_b / HEAD_DIM + EPS))
    return x * inv * g


def _kprep_kernel(k_ref, v_ref, g_ref, sl_ref, kaug_ref, kmean_ref, vt_ref, *, tq, tk):
    s_idx = pl.program_id(2)
    for grp in range(tq // tk):
        vblk = v_ref[grp * tk:(grp + 1) * tk, :].astype(F32)
        vt_ref[grp] = vblk.T.astype(BF16)
    kn = _head_pair_norm(k_ref[...].astype(F32), g_ref[...])
    nb = tq // MOBA_BLOCK
    km = jnp.mean(kn.reshape(nb, MOBA_BLOCK, LANES), axis=1)
    lane = lax.broadcasted_iota(I32, (tq, LANES), 1)
    row = lax.broadcasted_iota(I32, (tq, LANES), 0) + s_idx * tq
    blk = row // MOBA_BLOCK
    pos = row.astype(F32)
    lane_m = lax.broadcasted_iota(I32, (nb, LANES), 1)
    heads = ((kn, km), (pltpu.roll(kn, HEAD_DIM, axis=1), pltpu.roll(km, HEAD_DIM, axis=1)))
    for hh, (kk, kmm) in enumerate(heads):
        hi, mid, lo = _split3(sl_ref[hh:hh + 1, :] * pos)
        aug = jnp.where(lane < HEAD_DIM, kk, 0.0)
        aug = jnp.where((lane >= 64) & (lane < 96), (lane - 64 == blk).astype(F32), aug)
        aug = jnp.where(lane == 96, hi, aug)
        aug = jnp.where(lane == 97, mid, aug)
        aug = jnp.where(lane == 98, lo, aug)
        aug = jnp.where((lane >= 99) & (lane < 102), 1.0, aug)
        kaug_ref[hh] = aug.astype(BF16)
        kmean_ref[hh] = jnp.where(lane_m < HEAD_DIM, kmm, 0.0)


def _qprep_kernel(q_ref, g_ref, sl_ref, kmean_ref, qaug_ref, *, tq, nblk):
    s_idx = pl.program_id(2)
    qn = _head_pair_norm(q_ref[...].astype(F32), g_ref[...])
    lane = lax.broadcasted_iota(I32, (tq, LANES), 1)
    row = lax.broadcasted_iota(I32, (tq, LANES), 0) + s_idx * tq
    own = row // MOBA_BLOCK
    t = row.astype(F32)
    jl = lane - 64
    heads = (qn, pltpu.roll(qn, HEAD_DIM, axis=1))
    for hh, qh in enumerate(heads):
        qq = jnp.where(lane < HEAD_DIM, qh, 0.0)
        km_rows = jnp.concatenate(
            [jnp.zeros((64, LANES), F32), kmean_ref[hh], jnp.zeros((64 - nblk, LANES), F32)], axis=0)
        gate = lax.dot_general(qq, km_rows, NT_DIMS, precision=lax.Precision.HIGHEST,
                               preferred_element_type=F32)
        g = jnp.where((jl >= 0) & (jl < own), gate, -jnp.inf)
        allowed = jl == own
        for r in range(MOBA_TOPK):
            m = jnp.max(g, axis=-1, keepdims=True)
            first = jnp.min(jnp.where(g == m, lane, 1 << 20), axis=-1, keepdims=True)
            hit = lane == first
            allowed = allowed | (hit & (own > r))
            g = jnp.where(hit, -jnp.inf, g)
        hi, mid, lo = _split3(-sl_ref[hh:hh + 1, :] * t)
        aug = qq * (HEAD_DIM ** -0.5)
        aug = jnp.where((jl >= 0) & (jl < 32), jnp.where(allowed, 0.0, -MASK_BIG), aug)
        aug = jnp.where((lane >= 96) & (lane < 99), 1.0, aug)
        aug = jnp.where(lane == 99, hi, aug)
        aug = jnp.where(lane == 100, mid, aug)
        aug = jnp.where(lane == 101, lo, aug)
        qaug_ref[hh] = aug.astype(BF16)


def _attn_kernel(q_ref, k_ref, vt_ref, o_ref, acc_ref, m_ref, *, tq, tk):
    i = pl.program_id(2)
    vrow = lax.broadcasted_iota(I32, (LANES, tk), 0)
    low = vrow < HEAD_DIM
    last = (i * tq) // tk

    def group(g, diag):
        start = pl.multiple_of(g * tk, tk)
        vt = vt_ref[g]
        one = jnp.ones_like(vt)
        if diag:
            kpos = lax.broadcasted_iota(I32, (tk, tq), 0) + g * tk
            qpos = lax.broadcasted_iota(I32, (tk, tq), 1) + i * tq
            causal = kpos <= qpos
        for hh in range(2):
            kj = k_ref[hh, pl.ds(start, tk), :]
            s = lax.dot_general(kj, q_ref[hh], NT_DIMS, preferred_element_type=F32)
            if diag:
                s = jnp.where(causal, s, -MASK_BIG)
            m_old = m_ref[hh]
            m_new = jnp.maximum(m_old, jnp.max(s, axis=0, keepdims=True))
            alpha = jnp.exp(m_old - m_new)
            p = jnp.exp(s - m_new).astype(BF16)
            vaug = jnp.where(low, vt, one) if hh == 0 else jnp.where(low, one, vt)
            acc_ref[hh] = alpha * acc_ref[hh] + jnp.dot(vaug, p, preferred_element_type=F32)
            m_ref[hh] = m_new

    acc_ref[...] = jnp.zeros_like(acc_ref)
    m_ref[...] = jnp.full_like(m_ref, -MASK_BIG)
    group(last, True)

    def body(g, carry):
        group(g, False)
        return carry

    lax.fori_loop(0, last, body, 0)
    a = acc_ref[0]
    b = acc_ref[1]
    low_q = lax.broadcasted_iota(I32, (LANES, tq), 0) < HEAD_DIM
    out = jnp.where(low_q, a / a[HEAD_DIM:HEAD_DIM + 1, :], b / b[0:1, :])
    o_ref[...] = out.T.astype(o_ref.dtype)


def moba_attention(proj3, q_norm_g, k_norm_g):
    B, S, _ = proj3.shape
    nblk = S // MOBA_BLOCK
    assert S % MOBA_BLOCK == 0 and nblk <= 32
    HP = N_HEADS // 2
    tq = min(2048, S)
    slopes = jnp.exp2(-8.0 * jnp.arange(1, N_HEADS + 1, dtype=F32) / N_HEADS)
    sl = jnp.zeros((HP, SUBLANES, LANES), F32)
    sl = sl.at[:, 0, :].set(slopes[0::2, None]).at[:, 1, :].set(slopes[1::2, None])
    gq = jnp.tile(q_norm_g.astype(F32), 2)[None, :]
    gk = jnp.tile(k_norm_g.astype(F32), 2)[None, :]
    qb, kb, vb = COL_Q // LANES, COL_K // LANES, COL_V // LANES
    grid = (B, HP, S // tq)
    sem3 = ("parallel", "parallel", "parallel")

    nbt = tq // MOBA_BLOCK
    tk = MOBA_KEY_GROUP * MOBA_BLOCK
    assert tq % tk == 0
    kaug, kmean, vt = pl.pallas_call(
        functools.partial(_kprep_kernel, tq=tq, tk=tk),
        grid=grid,
        in_specs=[
            pl.BlockSpec((None, tq, LANES), lambda b, p, s: (b, s, kb + p)),
            pl.BlockSpec((None, tq, LANES), lambda b, p, s: (b, s, vb + p)),
            pl.BlockSpec((1, LANES), lambda b, p, s: (0, 0)),
            pl.BlockSpec((None, SUBLANES, LANES), lambda b, p, s: (p, 0, 0)),
        ],
        out_specs=[
            pl.BlockSpec((None, 2, tq, LANES), lambda b, p, s: (b, p, s, 0)),
            pl.BlockSpec((None, 2, nbt, LANES), lambda b, p, s: (b, p, s, 0)),
            pl.BlockSpec((None, None, tq // tk, LANES, tk), lambda b, p, s: (b, p, s, 0, 0)),
        ],
        out_shape=[
            jax.ShapeDtypeStruct((B, N_HEADS, S, LANES), BF16),
            jax.ShapeDtypeStruct((B, N_HEADS, nblk, LANES), F32),
            jax.ShapeDtypeStruct((B, HP, S // tk, LANES, tk), BF16),
        ],
        compiler_params=_cparams(sem3),
        name="moba_kprep",
    )(proj3, proj3, gk, sl)

    qaug = pl.pallas_call(
        functools.partial(_qprep_kernel, tq=tq, nblk=nblk),
        grid=grid,
        in_specs=[
            pl.BlockSpec((None, tq, LANES), lambda b, p, s: (b, s, qb + p)),
            pl.BlockSpec((1, LANES), lambda b, p, s: (0, 0)),
            pl.BlockSpec((None, SUBLANES, LANES), lambda b, p, s: (p, 0, 0)),
            pl.BlockSpec((None, 2, nblk, LANES), lambda b, p, s: (b, p, 0, 0)),
        ],
        out_specs=pl.BlockSpec((None, 2, tq, LANES), lambda b, p, s: (b, p, s, 0)),
        out_shape=jax.ShapeDtypeStruct((B, N_HEADS, S, LANES), BF16),
        compiler_params=_cparams(sem3),
        name="moba_qprep",
    )(proj3, gq, sl, kmean)

    ta = MOBA_BLOCK
    return pl.pallas_call(
        functools.partial(_attn_kernel, tq=ta, tk=tk),
        grid=(B, HP, S // ta),
        in_specs=[
            pl.BlockSpec((None, 2, ta, LANES), lambda b, p, i: (b, p, i, 0)),
            pl.BlockSpec((None, 2, S, LANES), lambda b, p, i: (b, p, 0, 0)),
            pl.BlockSpec((None, None, S // tk, LANES, tk), lambda b, p, i: (b, p, 0, 0, 0)),
        ],
        out_specs=pl.BlockSpec((None, ta, LANES), lambda b, p, i: (b, i, p)),
        out_shape=jax.ShapeDtypeStruct((B, S, ATTN_WIDTH), BF16),
        scratch_shapes=[pltpu.VMEM((2, LANES, ta), F32), pltpu.VMEM((2, 1, ta), F32)],
        compiler_params=_cparams(("parallel", "parallel", "arbitrary")),
        name="moba_attn",
    )(qaug, kaug, vt)


def _silu(x):
    return x * (1.0 / (1.0 + jnp.exp(-x)))


def _conv_silu(u_ref, halo_ref, w_ref, b_ref, first):
    u = u_ref[...].astype(F32)
    halo = jnp.where(first, 0.0, halo_ref[...].astype(F32))
    ext = jnp.concatenate([halo, u], axis=0)
    w = w_ref[...]
    out = b_ref[...] + w[3:4, :] * u
    for back in range(1, SSD_CONV):
        out = out + w[3 - back:4 - back, :] * pltpu.roll(ext, back, axis=0)[SUBLANES:, :]
    return _silu(out)


def _ssd_kernel(x_ref, xh_ref, b_ref, bh_ref, c_ref, ch_ref, z_ref, dt_ref,
                wx_ref, wb_ref, wc_ref, bx_ref, bb_ref, bc_ref, dtb_ref, rg_ref,
                a_ref, d_ref, ng_ref, y_ref, state_ref):
    c = pl.program_id(2)
    L = SSD_CHUNK
    W = SSD_GROUP_W
    first = c == 0

    @pl.when(first)
    def _():
        state_ref[...] = jnp.zeros_like(state_ref)

    xs = _conv_silu(x_ref, xh_ref, wx_ref, bx_ref, first)
    bm = _conv_silu(b_ref, bh_ref, wb_ref, bb_ref, first)
    cm = _conv_silu(c_ref, ch_ref, wc_ref, bc_ref, first)
    dt = jax.nn.softplus(dt_ref[...] + dtb_ref[...])
    hp = lax.Precision.HIGHEST
    dtx = jnp.dot(dt, rg_ref[...], precision=hp, preferred_element_type=F32)
    ax = dtx * a_ref[...]
    rr = lax.broadcasted_iota(I32, (L, L), 0)
    cc = lax.broadcasted_iota(I32, (L, L), 1)
    causal = cc <= rr
    acum = jnp.dot(causal.astype(F32), ax, precision=hp, preferred_element_type=F32)
    acum_t = acum.T
    a_last = acum[L - 1:L, :]
    xdt = xs * dtx
    cmb = cm.astype(BF16)
    cb = lax.dot_general(cmb, bm.astype(BF16), NT_DIMS, preferred_element_type=F32)
    lane = lax.broadcasted_iota(I32, (L, W), 1)
    y = jnp.zeros((L, W), F32)
    for r in range(W // 64):
        seg = acum[:, 64 * r:64 * r + 1] - acum_t[64 * r:64 * r + 1, :]
        lmat = jnp.exp(jnp.where(causal, seg, -jnp.inf))
        xr = jnp.where((lane >= 64 * r) & (lane < 64 * r + 64), xdt, 0.0).astype(BF16)
        y = y + jnp.dot((cb * lmat).astype(BF16), xr, preferred_element_type=F32)
    state = state_ref[...]
    y = y + jnp.dot(cmb, state.astype(BF16), preferred_element_type=F32) * jnp.exp(acum)
    wgt = (xdt * jnp.exp(a_last - acum)).astype(BF16)
    state_ref[...] = state * jnp.exp(a_last) + jnp.dot(bm.T.astype(BF16), wgt, preferred_element_type=F32)
    y = y + d_ref[...] * xs
    y = y * _silu(z_ref[...].astype(F32))
    y = y * lax.rsqrt(jnp.mean(y * y, axis=-1, keepdims=True) + EPS)
    y_ref[...] = (y * ng_ref[...]).astype(y_ref.dtype)


def ssd_mixer(proj3, dtraw3, conv_w, conv_b, dt_bias, a_log, d_skip, norm_g):
    B, S, _ = proj3.shape
    L, W, N, G = SSD_CHUNK, SSD_GROUP_W, SSD_STATE, SSD_GROUPS
    assert S % L == 0
    nc = S // L
    hb = L // SUBLANES
    xb, bb, cb_, zb = COL_X // W, COL_B // N, COL_C // N, COL_Z // W
    rep = SSD_INNER // SSD_HEADS
    a_exp = jnp.repeat(-jnp.exp(a_log.astype(F32)), rep)[None, :]
    d_exp = jnp.repeat(d_skip.astype(F32), rep)[None, :]
    ng = norm_g.astype(F32)[None, :]
    dtb = jnp.zeros((1, LANES), F32).at[0, :SSD_HEADS].set(dt_bias.astype(F32))
    head_of_chan = np.arange(SSD_INNER) // rep
    rg = (np.arange(LANES)[None, :, None] == head_of_chan.reshape(G, 1, W)).astype(np.float32)
    cw = conv_w.astype(F32)
    cbias = conv_b.astype(F32)[None, :]
    cxo, cbo, cco = 0, SSD_INNER // N, (SSD_INNER + SSD_GN) // N

    def halo(col):
        return lambda b, g, c: (b, jnp.maximum(c * hb - 1, 0), col + g)

    return pl.pallas_call(
        _ssd_kernel,
        grid=(B, G, nc),
        in_specs=[
            pl.BlockSpec((None, L, W), lambda b, g, c: (b, c, xb + g)),
            pl.BlockSpec((None, SUBLANES, W), halo(xb)),
            pl.BlockSpec((None, L, N), lambda b, g, c: (b, c, bb + g)),
            pl.BlockSpec((None, SUBLANES, N), halo(bb)),
            pl.BlockSpec((None, L, N), lambda b, g, c: (b, c, cb_ + g)),
            pl.BlockSpec((None, SUBLANES, N), halo(cb_)),
            pl.BlockSpec((None, L, W), lambda b, g, c: (b, c, zb + g)),
            pl.BlockSpec((None, L, LANES), lambda b, g, c: (b, c, 0)),
            pl.BlockSpec((SSD_CONV, W), lambda b, g, c: (0, cxo + g)),
            pl.BlockSpec((SSD_CONV, N), lambda b, g, c: (0, cbo + g)),
            pl.BlockSpec((SSD_CONV, N), lambda b, g, c: (0, cco + g)),
            pl.BlockSpec((1, W), lambda b, g, c: (0, cxo + g)),
            pl.BlockSpec((1, N), lambda b, g, c: (0, cbo + g)),
            pl.BlockSpec((1, N), lambda b, g, c: (0, cco + g)),
            pl.BlockSpec((1, LANES), lambda b, g, c: (0, 0)),
            pl.BlockSpec((None, LANES, W), lambda b, g, c: (g, 0, 0)),
            pl.BlockSpec((1, W), lambda b, g, c: (0, g)),
            pl.BlockSpec((1, W), lambda b, g, c: (0, g)),
            pl.BlockSpec((1, W), lambda b, g, c: (0, g)),
        ],
        out_specs=pl.BlockSpec((None, L, W), lambda b, g, c: (b, c, g)),
        out_shape=jax.ShapeDtypeStruct((B, S, SSD_INNER), BF16),
        scratch_shapes=[pltpu.VMEM((N, W), F32)],
        compiler_params=_cparams(("parallel", "parallel", "arbitrary")),
        name="ssd_scan",
    )(proj3, proj3, proj3, proj3, proj3, proj3, proj3, dtraw3,
      cw, cw, cw, cbias, cbias, cbias, dtb, jnp.asarray(rg), a_exp, d_exp, ng)


def _merge_kernel(x_ref, a_ref, s_ref, ga_ref, gs_ref, wa_ref, ws_ref, wo_ref, o_ref):
    ya = jnp.dot(a_ref[...], wa_ref[...], preferred_element_type=F32)
    ys = jnp.dot(s_ref[...], ws_ref[...], preferred_element_type=F32)
    mixed = jax.nn.sigmoid(ga_ref[...].astype(F32)) * ya + jax.nn.sigmoid(gs_ref[...].astype(F32)) * ys
    o_ref[...] = x_ref[...] + jnp.dot(mixed.astype(BF16), wo_ref[...], preferred_element_type=F32)


def merge_branches(x2, attn2, ssd2, proj2, wa, ws, wo, tm):
    T, D = x2.shape
    full = lambda a: pl.BlockSpec(a.shape, lambda i: (0, 0))
    return pl.pallas_call(
        _merge_kernel,
        grid=(T // tm,),
        in_specs=[
            pl.BlockSpec((tm, D), lambda i: (i, 0)),
            pl.BlockSpec((tm, ATTN_WIDTH), lambda i: (i, 0)),
            pl.BlockSpec((tm, SSD_INNER), lambda i: (i, 0)),
            pl.BlockSpec((tm, D), lambda i: (i, COL_GA // D)),
            pl.BlockSpec((tm, D), lambda i: (i, COL_GS // D)),
            full(wa), full(ws), full(wo),
        ],
        out_specs=pl.BlockSpec((tm, D), lambda i: (i, 0)),
        out_shape=jax.ShapeDtypeStruct((T, D), F32),
        compiler_params=_cparams(("parallel",)),
        name="merge",
    )(x2, attn2, ssd2, proj2, proj2, wa, ws, wo)


def _topk_rows(s, idx, k):
    n = s.shape[0]
    row = lax.broadcasted_iota(I32, s.shape, 0)
    vals, rows, picked = [], [], []
    for _ in range(k):
        m = jnp.max(s, axis=0, keepdims=True)
        first = jnp.min(jnp.where(s == m, row, n), axis=0, keepdims=True)
        hit = row == first
        vals.append(m)
        rows.append(first)
        if idx is not None:
            picked.append(jnp.max(jnp.where(hit, idx, -1), axis=0, keepdims=True))
        s = jnp.where(hit, -jnp.inf, s)
    cat = lambda xs: jnp.concatenate(xs, axis=0)
    return cat(vals), cat(rows), (cat(picked) if idx is not None else None)


def _peer_topk_kernel(q_ref, k1_ref, k2_ref, eidx_ref, gw_ref, *, tt):
    half = PEER_QDIM // 2
    e_rows, g_rows = [], []
    for h in range(PEER_HEADS):
        qa = q_ref[:, h * PEER_QDIM:h * PEER_QDIM + half].astype(BF16)
        qb = q_ref[:, h * PEER_QDIM + half:(h + 1) * PEER_QDIM].astype(BF16)
        s1 = lax.dot_general(k1_ref[h], qa, NT_DIMS, preferred_element_type=F32)
        s2 = lax.dot_general(k2_ref[h], qb, NT_DIMS, preferred_element_type=F32)
        v1, i1, _ = _topk_rows(s1, None, PEER_TOPK)
        v2, i2, _ = _topk_rows(s2, None, PEER_TOPK)
        cand = jnp.concatenate([v1[a:a + 1, :] + v2 for a in range(PEER_TOPK)], axis=0)
        cidx = jnp.concatenate([i1[a:a + 1, :] * PEER_NKEYS + i2 for a in range(PEER_TOPK)], axis=0)
        sv, _, ex = _topk_rows(cand, cidx, PEER_TOPK)
        e = jnp.exp(sv - sv[0:1, :])
        g_rows.append(e / jnp.sum(e, axis=0, keepdims=True))
        e_rows.append(ex)
    eidx_ref[...] = jnp.concatenate(e_rows, axis=0).T
    gw = jnp.concatenate(g_rows, axis=0)
    for part in range(tt // LANES):
        gw_ref[part] = gw[:, part * LANES:(part + 1) * LANES]


def peer_topk(q, keys1, keys2, tt):
    T = q.shape[0]
    full3 = lambda a: pl.BlockSpec(a.shape, lambda i: (0, 0, 0))
    return pl.pallas_call(
        functools.partial(_peer_topk_kernel, tt=tt),
        grid=(T // tt,),
        in_specs=[pl.BlockSpec((tt, q.shape[1]), lambda i: (i, 0)), full3(keys1), full3(keys2)],
        out_specs=[
            pl.BlockSpec((tt, PEER_SLOTS), lambda i: (i, 0)),
            pl.BlockSpec((tt // LANES, PEER_SLOTS, LANES), lambda i: (i, 0, 0)),
        ],
        out_shape=[
            jax.ShapeDtypeStruct((T, PEER_SLOTS), I32),
            jax.ShapeDtypeStruct((T // LANES, PEER_SLOTS, LANES), F32),
        ],
        compiler_params=_cparams(("parallel",)),
        name="peer_topk",
    )(q, keys1, keys2)


def pack_table(tab):
    half = tab.shape[1] // 2
    bits = lax.bitcast_convert_type(tab.astype(BF16), jnp.uint16).astype(jnp.uint32)
    return lax.bitcast_convert_type((bits[:, :half] << 16) | bits[:, half:], I32)


def sc_gather_rows(table, idx, win=64):
    V, W = table.shape
    N = idx.shape[0]
    info = plsc.get_sparse_core_info()
    n_cores, n_sub = info.num_cores, info.num_subcores
    workers = n_cores * n_sub
    per_w = N // workers
    assert per_w * workers == N and per_w % win == 0
    steps = per_w // win
    mesh = plsc.VectorSubcoreMesh(core_axis_name="c", subcore_axis_name="s")

    @functools.partial(
        pl.kernel, mesh=mesh,
        out_type=jax.ShapeDtypeStruct((N, W), table.dtype),
        scratch_types=[
            pltpu.VMEM((win,), I32),
            pltpu.VMEM((win, W), table.dtype),
            pltpu.SemaphoreType.DMA,
        ],
    )
    def gather_kernel(table_hbm, idx_hbm, out_hbm, idx_v, rows_v, sem):
        wid = lax.axis_index("s") * n_cores + lax.axis_index("c")
        base = wid * per_w

        @pl.loop(0, steps)
        def _(i):
            off = pl.multiple_of(base + i * win, SUBLANES)
            pltpu.sync_copy(idx_hbm.at[pl.ds(off, win)], idx_v)
            pltpu.async_copy(table_hbm.at[idx_v], rows_v, sem).wait()
            pltpu.sync_copy(rows_v, out_hbm.at[pl.ds(off, win)])

    return gather_kernel(table, idx)


def _unpack_words(w):
    u = pltpu.bitcast(w, jnp.uint32)
    hi = pltpu.bitcast(u & jnp.uint32(0xFFFF0000), F32)
    lo = pltpu.bitcast(u << 16, F32)
    return hi, lo


def _peer_expert_kernel(x_ref, g_ref, ug_ref, vg_ref, gw_ref, o_ref, *, tt):
    i = pl.program_id(0)
    half = D_MODEL // 2
    x1 = x_ref[...]
    xn = x1 * lax.rsqrt(jnp.mean(x1 * x1, axis=-1, keepdims=True) + EPS) * g_ref[...]
    lane = lax.broadcasted_iota(I32, (PEER_SLOTS, LANES), 1)
    off = (i % (LANES // tt)) * tt
    act = jnp.zeros((PEER_SLOTS, LANES), F32)
    for t in range(tt):
        hi, lo = _unpack_words(ug_ref[t * PEER_SLOTS:(t + 1) * PEER_SLOTS, :])
        prod = hi * xn[t:t + 1, :half] + lo * xn[t:t + 1, half:]
        fold = prod[:, 0:LANES]
        for c in range(1, half // LANES):
            fold = fold + prod[:, c * LANES:(c + 1) * LANES]
        col = jnp.sum(fold, axis=-1, keepdims=True)
        act = jnp.where(lane == off + t, col, act)
    gelu = 0.5 * act * (1.0 + lax.erf(act * (2.0 ** -0.5)))
    hact = gelu * gw_ref[...]
    for t in range(tt):
        hcol = jnp.sum(jnp.where(lane == off + t, hact, 0.0), axis=-1, keepdims=True)
        hi, lo = _unpack_words(vg_ref[t * PEER_SLOTS:(t + 1) * PEER_SLOTS, :])
        o_ref[t:t + 1, :half] = x1[t:t + 1, :half] + jnp.sum(hcol * hi, axis=0, keepdims=True)
        o_ref[t:t + 1, half:] = x1[t:t + 1, half:] + jnp.sum(hcol * lo, axis=0, keepdims=True)


def peer_experts(x1, g2, ug, vg, gw, tt=16):
    T, D = x1.shape
    W = ug.shape[1]
    per = LANES // tt
    return pl.pallas_call(
        functools.partial(_peer_expert_kernel, tt=tt),
        grid=(T // tt,),
        in_specs=[
            pl.BlockSpec((tt, D), lambda i: (i, 0)),
            pl.BlockSpec((1, D), lambda i: (0, 0)),
            pl.BlockSpec((tt * PEER_SLOTS, W), lambda i: (i, 0)),
            pl.BlockSpec((tt * PEER_SLOTS, W), lambda i: (i, 0)),
            pl.BlockSpec((None, PEER_SLOTS, LANES), lambda i: (i // per, 0, 0)),
        ],
        out_specs=pl.BlockSpec((tt, D), lambda i: (i, 0)),
        out_shape=jax.ShapeDtypeStruct((T, D), F32),
        compiler_params=_cparams(("parallel",)),
        name="peer_experts",
    )(x1, g2, ug, vg, gw)


def peer_ffn_residual(x1, norm2_g, wq, keys1, keys2, u_packed, v_packed):
    T = x1.shape[0]
    g2 = norm2_g.astype(F32)[None, :]
    q = norm_matmul(x1, g2, wq, F32, min(1024, T), 512)
    eidx, gw = peer_topk(q, keys1, keys2, min(256, T))
    flat = eidx.reshape(-1)
    ug = sc_gather_rows(u_packed, flat)
    vg = sc_gather_rows(v_packed, flat)
    return peer_experts(x1, g2, ug, vg, gw)


def kernel(x, norm1_g, w_in, q_norm_g, k_norm_g, conv_w, conv_b, dt_bias, a_log, d_skip, ssd_norm_g,
           w_attn_o, w_ssd_o, w_out, norm2_g, w_peer_q, peer_keys1, peer_keys2, peer_u, peer_v):
    B, S, D = x.shape
    xs = [x[b] for b in range(B)]
    for l in range(norm1_g.shape[0]):
        w = w_in[l]
        dt0 = COL_GA
        w_main = jnp.concatenate([w[:, :dt0], w[:, dt0 + SSD_HEADS:]], axis=1).astype(BF16)
        w_dt = jnp.zeros((D, LANES), BF16).at[:, :SSD_HEADS].set(w[:, dt0:dt0 + SSD_HEADS].astype(BF16))
        g1 = norm1_g[l].astype(F32)[None, :]
        wa, ws, wo = w_attn_o[l].astype(BF16), w_ssd_o[l].astype(BF16), w_out[l].astype(BF16)
        wq = w_peer_q[l].astype(BF16)
        k1, k2 = peer_keys1[l].astype(BF16), peer_keys2[l].astype(BF16)
        up, vp = pack_table(peer_u[l]), pack_table(peer_v[l])
        tm = min(1024, S)
        for b in range(B):
            x2 = xs[b]
            proj = norm_matmul(x2, g1, w_main, BF16, tm, 512)
            dtraw = norm_matmul(x2, g1, w_dt, F32, tm, LANES)
            proj3 = proj[None]
            attn = moba_attention(proj3, q_norm_g[l], k_norm_g[l])
            yssd = ssd_mixer(proj3, dtraw[None], conv_w[l], conv_b[l], dt_bias[l],
                             a_log[l], d_skip[l], ssd_norm_g[l])
            x1 = merge_branches(x2, attn[0], yssd[0], proj, wa, ws, wo, min(512, S))
            xs[b] = peer_ffn_residual(x1, norm2_g[l], wq, k1, k2, up, vp)
    return jnp.stack(xs, axis=0)
```

```python
import functools

import jax
import jax.numpy as jnp
import numpy as np
from jax import lax
from jax.experimental import pallas as pl
from jax.experimental.pallas import tpu as pltpu
from jax.experimental.pallas import tpu_sc as plsc

F32 = jnp.float32
BF16 = jnp.bfloat16
I32 = jnp.int32

EPS = 1e-6
D_MODEL = 1024
N_HEADS = 16
HEAD_DIM = 64
ATTN_WIDTH = N_HEADS * HEAD_DIM
MOBA_BLOCK = 256
MOBA_TOPK = 3
MOBA_KEY_GROUP = 4
MOBA_Q_TILE = 1024
SSD_INNER = 2048
SSD_HEADS = 32
SSD_GROUPS = 8
SSD_STATE = 128
SSD_CONV = 4
SSD_CHUNK = 256
SSD_GN = SSD_GROUPS * SSD_STATE
SSD_GROUP_W = SSD_INNER // SSD_GROUPS
PEER_HEADS = 8
PEER_NKEYS = 128
PEER_QDIM = 256
PEER_TOPK = 16
PEER_SLOTS = PEER_HEADS * PEER_TOPK

LANES = 128
SUBLANES = 8
VMEM_LIMIT = 56 * 1024 * 1024
MASK_BIG = 1e30

COL_Q, COL_K, COL_V = 0, ATTN_WIDTH, 2 * ATTN_WIDTH
COL_Z = 3 * ATTN_WIDTH
COL_X = COL_Z + SSD_INNER
COL_B = COL_X + SSD_INNER
COL_C = COL_B + SSD_GN
COL_GA = COL_C + SSD_GN
COL_GS = COL_GA + D_MODEL
PROJ_COLS = COL_GS + D_MODEL

NT_DIMS = (((1,), (1,)), ((), ()))


def _cparams(sem):
    return pltpu.CompilerParams(dimension_semantics=sem, vmem_limit_bytes=VMEM_LIMIT)


def _norm_matmul_kernel(x_ref, g_ref, w_ref, o_ref, h_ref):
    @pl.when(pl.program_id(1) == 0)
    def _():
        x = x_ref[...]
        y = x * lax.rsqrt(jnp.mean(x * x, axis=-1, keepdims=True) + EPS)
        h_ref[...] = (y * g_ref[...]).astype(h_ref.dtype)

    o_ref[...] = jnp.dot(h_ref[...], w_ref[...], preferred_element_type=F32).astype(o_ref.dtype)


def norm_matmul(x, g, w, out_dtype, tm, tn):
    T, K = x.shape
    N = w.shape[1]
    return pl.pallas_call(
        _norm_matmul_kernel,
        grid=(T // tm, N // tn),
        in_specs=[
            pl.BlockSpec((tm, K), lambda i, j: (i, 0)),
            pl.BlockSpec((1, K), lambda i, j: (0, 0)),
            pl.BlockSpec((K, tn), lambda i, j: (0, j)),
        ],
        out_specs=pl.BlockSpec((tm, tn), lambda i, j: (i, j)),
        out_shape=jax.ShapeDtypeStruct((T, N), out_dtype),
        scratch_shapes=[pltpu.VMEM((tm, K), BF16)],
        compiler_params=_cparams(("parallel", "arbitrary")),
        name="norm_matmul",
    )(x, g, w)


def _split3(v):
    hi = v.astype(BF16).astype(F32)
    r1 = v - hi
    mid = r1.astype(BF16).astype(F32)
    return hi, mid, r1 - mid


def _head_pair_norm(x, g):
    lane = lax.broadcasted_iota(I32, x.shape, 1)
    low = lane < HEAD_DIM
    x2 = x * x
    ss_a = jnp.sum(jnp.where(low, x2, 0.0), axis=-1, keepdims=True)
    ss_b = jnp.sum(jnp.where(low, 0.0, x2), axis=-1, keepdims=True)
    inv = jnp.where(low, lax.rsqrt(ss_a / HEAD_DIM + EPS), lax.rsqrt(ss_b / HEAD_DIM + EPS))
    return x * inv * g


def _kprep_kernel(k_ref, v_ref, g_ref, sl_ref, kaug_ref, kmean_ref, vt_ref, *, tq, tk):
    s_idx = pl.program_id(2)
    for grp in range(tq // tk):
        vblk = v_ref[grp * tk:(grp + 1) * tk, :].astype(F32)
        vt_ref[grp] = vblk.T.astype(BF16)
    kn = _head_pair_norm(k_ref[...].astype(F32), g_ref[...])
    nb = tq // MOBA_BLOCK
    km = jnp.mean(kn.reshape(nb, MOBA_BLOCK, LANES), axis=1)
    lane = lax.broadcasted_iota(I32, (tq, LANES), 1)
    row = lax.broadcasted_iota(I32, (tq, LANES), 0) + s_idx * tq
    blk = row // MOBA_BLOCK
    pos = row.astype(F32)
    lane_m = lax.broadcasted_iota(I32, (nb, LANES), 1)
    heads = ((kn, km), (pltpu.roll(kn, HEAD_DIM, axis=1), pltpu.roll(km, HEAD_DIM, axis=1)))
    for hh, (kk, kmm) in enumerate(heads):
        hi, mid, lo = _split3(sl_ref[hh:hh + 1, :] * pos)
        aug = jnp.where(lane < HEAD_DIM, kk, 0.0)
        aug = jnp.where((lane >= 64) & (lane < 96), (lane - 64 == blk).astype(F32), aug)
        aug = jnp.where(lane == 96, hi, aug)
        aug = jnp.where(lane == 97, mid, aug)
        aug = jnp.where(lane == 98, lo, aug)
        aug = jnp.where((lane >= 99) & (lane < 102), 1.0, aug)
        kaug_ref[hh] = aug.astype(BF16)
        kmean_ref[hh] = jnp.where(lane_m < HEAD_DIM, kmm, 0.0)


def _qprep_kernel(q_ref, g_ref, sl_ref, kmean_ref, qaug_ref, *, tq, nblk):
    s_idx = pl.program_id(2)
    qn = _head_pair_norm(q_ref[...].astype(F32), g_ref[...])
    lane = lax.broadcasted_iota(I32, (tq, LANES), 1)
    row = lax.broadcasted_iota(I32, (tq, LANES), 0) + s_idx * tq
    own = row // MOBA_BLOCK
    t = row.astype(F32)
    jl = lane - 64
    heads = (qn, pltpu.roll(qn, HEAD_DIM, axis=1))
    for hh, qh in enumerate(heads):
        qq = jnp.where(lane < HEAD_DIM, qh, 0.0)
        km_rows = jnp.concatenate(
            [jnp.zeros((64, LANES), F32), kmean_ref[hh], jnp.zeros((64 - nblk, LANES), F32)], axis=0)
        gate = lax.dot_general(qq, km_rows, NT_DIMS, precision=lax.Precision.HIGHEST,
                               preferred_element_type=F32)
        g = jnp.where((jl >= 0) & (jl < own), gate, -jnp.inf)
        allowed = jl == own
        for r in range(MOBA_TOPK):
            m = jnp.max(g, axis=-1, keepdims=True)
            first = jnp.min(jnp.where(g == m, lane, 1 << 20), axis=-1, keepdims=True)
            hit = lane == first
            allowed = allowed | (hit & (own > r))
            g = jnp.where(hit, -jnp.inf, g)
        hi, mid, lo = _split3(-sl_ref[hh:hh + 1, :] * t)
        aug = qq * (HEAD_DIM ** -0.5)
        aug = jnp.where((jl >= 0) & (jl < 32), jnp.where(allowed, 0.0, -MASK_BIG), aug)
        aug = jnp.where((lane >= 96) & (lane < 99), 1.0, aug)
        aug = jnp.where(lane == 99, hi, aug)
        aug = jnp.where(lane == 100, mid, aug)
        aug = jnp.where(lane == 101, lo, aug)
        qaug_ref[hh] = aug.astype(BF16)


def _attn_kernel(q_ref, k_ref, vt_ref, o_ref, acc_ref, m_ref, *, tq, tk):
    i = pl.program_id(2)
    vrow = lax.broadcasted_iota(I32, (LANES, tk), 0)
    low = vrow < HEAD_DIM
    last = (i * tq) // tk

    def group(g, diag):
        start = pl.multiple_of(g * tk, tk)
        vt = vt_ref[g]
        one = jnp.ones_like(vt)
        if diag:
            kpos = lax.broadcasted_iota(I32, (tk, tq), 0) + g * tk
            qpos = lax.broadcasted_iota(I32, (tk, tq), 1) + i * tq
            causal = kpos <= qpos
        for hh in range(2):
            kj = k_ref[hh, pl.ds(start, tk), :]
            s = lax.dot_general(kj, q_ref[hh], NT_DIMS, preferred_element_type=F32)
            if diag:
                s = jnp.where(causal, s, -MASK_BIG)
            m_old = m_ref[hh]
            m_new = jnp.maximum(m_old, jnp.max(s, axis=0, keepdims=True))
            alpha = jnp.exp(m_old - m_new)
            p = jnp.exp(s - m_new).astype(BF16)
            vaug = jnp.where(low, vt, one) if hh == 0 else jnp.where(low, one, vt)
            acc_ref[hh] = alpha * acc_ref[hh] + jnp.dot(vaug, p, preferred_element_type=F32)
            m_ref[hh] = m_new

    acc_ref[...] = jnp.zeros_like(acc_ref)
    m_ref[...] = jnp.full_like(m_ref, -MASK_BIG)
    group(last, True)

    def body(g, carry):
        group(g, False)
        return carry

    lax.fori_loop(0, last, body, 0)
    a = acc_ref[0]
    b = acc_ref[1]
    low_q = lax.broadcasted_iota(I32, (LANES, tq), 0) < HEAD_DIM
    out = jnp.where(low_q, a / a[HEAD_DIM:HEAD_DIM + 1, :], b / b[0:1, :])
    o_ref[...] = out.T.astype(o_ref.dtype)


def moba_attention(proj3, q_norm_g, k_norm_g):
    B, S, _ = proj3.shape
    nblk = S // MOBA_BLOCK
    assert S % MOBA_BLOCK == 0 and nblk <= 32
    HP = N_HEADS // 2
    tq = min(2048, S)
    slopes = jnp.exp2(-8.0 * jnp.arange(1, N_HEADS + 1, dtype=F32) / N_HEADS)
    sl = jnp.zeros((HP, SUBLANES, LANES), F32)
    sl = sl.at[:, 0, :].set(slopes[0::2, None]).at[:, 1, :].set(slopes[1::2, None])
    gq = jnp.tile(q_norm_g.astype(F32), 2)[None, :]
    gk = jnp.tile(k_norm_g.astype(F32), 2)[None, :]
    qb, kb, vb = COL_Q // LANES, COL_K // LANES, COL_V // LANES
    grid = (B, HP, S // tq)
    sem3 = ("parallel", "parallel", "parallel")

    nbt = tq // MOBA_BLOCK
    tk = MOBA_KEY_GROUP * MOBA_BLOCK
    assert tq % tk == 0
    kaug, kmean, vt = pl.pallas_call(
        functools.partial(_kprep_kernel, tq=tq, tk=tk),
        grid=grid,
        in_specs=[
            pl.BlockSpec((None, tq, LANES), lambda b, p, s: (b, s, kb + p)),
            pl.BlockSpec((None, tq, LANES), lambda b, p, s: (b, s, vb + p)),
            pl.BlockSpec((1, LANES), lambda b, p, s: (0, 0)),
            pl.BlockSpec((None, SUBLANES, LANES), lambda b, p, s: (p, 0, 0)),
        ],
        out_specs=[
            pl.BlockSpec((None, 2, tq, LANES), lambda b, p, s: (b, p, s, 0)),
            pl.BlockSpec((None, 2, nbt, LANES), lambda b, p, s: (b, p, s, 0)),
            pl.BlockSpec((None, None, tq // tk, LANES, tk), lambda b, p, s: (b, p, s, 0, 0)),
        ],
        out_shape=[
            jax.ShapeDtypeStruct((B, N_HEADS, S, LANES), BF16),
            jax.ShapeDtypeStruct((B, N_HEADS, nblk, LANES), F32),
            jax.ShapeDtypeStruct((B, HP, S // tk, LANES, tk), BF16),
        ],
        compiler_params=_cparams(sem3),
        name="moba_kprep",
    )(proj3, proj3, gk, sl)

    qaug = pl.pallas_call(
        functools.partial(_qprep_kernel, tq=tq, nblk=nblk),
        grid=grid,
        in_specs=[
            pl.BlockSpec((None, tq, LANES), lambda b, p, s: (b, s, qb + p)),
            pl.BlockSpec((1, LANES), lambda b, p, s: (0, 0)),
            pl.BlockSpec((None, SUBLANES, LANES), lambda b, p, s: (p, 0, 0)),
            pl.BlockSpec((None, 2, nblk, LANES), lambda b, p, s: (b, p, 0, 0)),
        ],
        out_specs=pl.BlockSpec((None, 2, tq, LANES), lambda b, p, s: (b, p, s, 0)),
        out_shape=jax.ShapeDtypeStruct((B, N_HEADS, S, LANES), BF16),
        compiler_params=_cparams(sem3),
        name="moba_qprep",
    )(proj3, gq, sl, kmean)

    ta = min(MOBA_Q_TILE, S)
    assert tk % ta == 0
    return pl.pallas_call(
        functools.partial(_attn_kernel, tq=ta, tk=tk),
        grid=(B, HP, S // ta),
        in_specs=[
            pl.BlockSpec((None, 2, ta, LANES), lambda b, p, i: (b, p, i, 0)),
            pl.BlockSpec((None, 2, S, LANES), lambda b, p, i: (b, p, 0, 0)),
            pl.BlockSpec((None, None, S // tk, LANES, tk), lambda b, p, i: (b, p, 0, 0, 0)),
        ],
        out_specs=pl.BlockSpec((None, ta, LANES), lambda b, p, i: (b, i, p)),
        out_shape=jax.ShapeDtypeStruct((B, S, ATTN_WIDTH), BF16),
        scratch_shapes=[pltpu.VMEM((2, LANES, ta), F32), pltpu.VMEM((2, 1, ta), F32)],
        compiler_params=_cparams(("parallel", "parallel", "arbitrary")),
        name="moba_attn",
    )(qaug, kaug, vt)


def _silu(x):
    return x * (1.0 / (1.0 + jnp.exp(-x)))


def _conv_silu(u_ref, halo_ref, w_ref, b_ref, first):
    u = u_ref[...].astype(F32)
    halo = jnp.where(first, 0.0, halo_ref[...].astype(F32))
    ext = jnp.concatenate([halo, u], axis=0)
    w = w_ref[...]
    out = b_ref[...] + w[3:4, :] * u
    for back in range(1, SSD_CONV):
        out = out + w[3 - back:4 - back, :] * pltpu.roll(ext, back, axis=0)[SUBLANES:, :]
    return _silu(out)


def _ssd_kernel(x_ref, xh_ref, b_ref, bh_ref, c_ref, ch_ref, z_ref, dt_ref,
                wx_ref, wb_ref, wc_ref, bx_ref, bb_ref, bc_ref, dtb_ref, rg_ref,
                a_ref, d_ref, ng_ref, y_ref, state_ref):
    c = pl.program_id(2)
    L = SSD_CHUNK
    W = SSD_GROUP_W
    first = c == 0

    @pl.when(first)
    def _():
        state_ref[...] = jnp.zeros_like(state_ref)

    xs = _conv_silu(x_ref, xh_ref, wx_ref, bx_ref, first)
    bm = _conv_silu(b_ref, bh_ref, wb_ref, bb_ref, first)
    cm = _conv_silu(c_ref, ch_ref, wc_ref, bc_ref, first)
    dt = jax.nn.softplus(dt_ref[...] + dtb_ref[...])
    hp = lax.Precision.HIGHEST
    dtx = jnp.dot(dt, rg_ref[...], precision=hp, preferred_element_type=F32)
    ax = dtx * a_ref[...]
    rr = lax.broadcasted_iota(I32, (L, L), 0)
    cc = lax.broadcasted_iota(I32, (L, L), 1)
    causal = cc <= rr
    acum = jnp.dot(causal.astype(F32), ax, precision=hp, preferred_element_type=F32)
    acum_t = acum.T
    a_last = acum[L - 1:L, :]
    xdt = xs * dtx
    cmb = cm.astype(BF16)
    cb = lax.dot_general(cmb, bm.astype(BF16), NT_DIMS, preferred_element_type=F32)
    lane = lax.broadcasted_iota(I32, (L, W), 1)
    y = jnp.zeros((L, W), F32)
    for r in range(W // 64):
        seg = acum[:, 64 * r:64 * r + 1] - acum_t[64 * r:64 * r + 1, :]
        lmat = jnp.exp(jnp.where(causal, seg, -jnp.inf))
        xr = jnp.where((lane >= 64 * r) & (lane < 64 * r + 64), xdt, 0.0).astype(BF16)
        y = y + jnp.dot((cb * lmat).astype(BF16), xr, preferred_element_type=F32)
    state = state_ref[...]
    y = y + jnp.dot(cmb, state.astype(BF16), preferred_element_type=F32) * jnp.exp(acum)
    wgt = (xdt * jnp.exp(a_last - acum)).astype(BF16)
    state_ref[...] = state * jnp.exp(a_last) + jnp.dot(bm.T.astype(BF16), wgt, preferred_element_type=F32)
    y = y + d_ref[...] * xs
    y = y * _silu(z_ref[...].astype(F32))
    y = y * lax.rsqrt(jnp.mean(y * y, axis=-1, keepdims=True) + EPS)
    y_ref[...] = (y * ng_ref[...]).astype(y_ref.dtype)


def ssd_mixer(proj3, dtraw3, conv_w, conv_b, dt_bias, a_log, d_skip, norm_g):
    B, S, _ = proj3.shape
    L, W, N, G = SSD_CHUNK, SSD_GROUP_W, SSD_STATE, SSD_GROUPS
    assert S % L == 0
    nc = S // L
    hb = L // SUBLANES
    xb, bb, cb_, zb = COL_X // W, COL_B // N, COL_C // N, COL_Z // W
    rep = SSD_INNER // SSD_HEADS
    a_exp = jnp.repeat(-jnp.exp(a_log.astype(F32)), rep)[None, :]
    d_exp = jnp.repeat(d_skip.astype(F32), rep)[None, :]
    ng = norm_g.astype(F32)[None, :]
    dtb = jnp.zeros((1, LANES), F32).at[0, :SSD_HEADS].set(dt_bias.astype(F32))
    head_of_chan = np.arange(SSD_INNER) // rep
    rg = (np.arange(LANES)[None, :, None] == head_of_chan.reshape(G, 1, W)).astype(np.float32)
    cw = conv_w.astype(F32)
    cbias = conv_b.astype(F32)[None, :]
    cxo, cbo, cco = 0, SSD_INNER // N, (SSD_INNER + SSD_GN) // N

    def halo(col):
        return lambda b, g, c: (b, jnp.maximum(c * hb - 1, 0), col + g)

    return pl.pallas_call(
        _ssd_kernel,
        grid=(B, G, nc),
        in_specs=[
            pl.BlockSpec((None, L, W), lambda b, g, c: (b, c, xb + g)),
            pl.BlockSpec((None, SUBLANES, W), halo(xb)),
            pl.BlockSpec((None, L, N), lambda b, g, c: (b, c, bb + g)),
            pl.BlockSpec((None, SUBLANES, N), halo(bb)),
            pl.BlockSpec((None, L, N), lambda b, g, c: (b, c, cb_ + g)),
            pl.BlockSpec((None, SUBLANES, N), halo(cb_)),
            pl.BlockSpec((None, L, W), lambda b, g, c: (b, c, zb + g)),
            pl.BlockSpec((None, L, LANES), lambda b, g, c: (b, c, 0)),
            pl.BlockSpec((SSD_CONV, W), lambda b, g, c: (0, cxo + g)),
            pl.BlockSpec((SSD_CONV, N), lambda b, g, c: (0, cbo + g)),
            pl.BlockSpec((SSD_CONV, N), lambda b, g, c: (0, cco + g)),
            pl.BlockSpec((1, W), lambda b, g, c: (0, cxo + g)),
            pl.BlockSpec((1, N), lambda b, g, c: (0, cbo + g)),
            pl.BlockSpec((1, N), lambda b, g, c: (0, cco + g)),
            pl.BlockSpec((1, LANES), lambda b, g, c: (0, 0)),
            pl.BlockSpec((None, LANES, W), lambda b, g, c: (g, 0, 0)),
            pl.BlockSpec((1, W), lambda b, g, c: (0, g)),
            pl.BlockSpec((1, W), lambda b, g, c: (0, g)),
            pl.BlockSpec((1, W), lambda b, g, c: (0, g)),
        ],
        out_specs=pl.BlockSpec((None, L, W), lambda b, g, c: (b, c, g)),
        out_shape=jax.ShapeDtypeStruct((B, S, SSD_INNER), BF16),
        scratch_shapes=[pltpu.VMEM((N, W), F32)],
        compiler_params=_cparams(("parallel", "parallel", "arbitrary")),
        name="ssd_scan",
    )(proj3, proj3, proj3, proj3, proj3, proj3, proj3, dtraw3,
      cw, cw, cw, cbias, cbias, cbias, dtb, jnp.asarray(rg), a_exp, d_exp, ng)


def _merge_kernel(x_ref, a_ref, s_ref, ga_ref, gs_ref, wa_ref, ws_ref, wo_ref, o_ref):
    ya = jnp.dot(a_ref[...], wa_ref[...], preferred_element_type=F32)
    ys = jnp.dot(s_ref[...], ws_ref[...], preferred_element_type=F32)
    mixed = jax.nn.sigmoid(ga_ref[...].astype(F32)) * ya + jax.nn.sigmoid(gs_ref[...].astype(F32)) * ys
    o_ref[...] = x_ref[...] + jnp.dot(mixed.astype(BF16), wo_ref[...], preferred_element_type=F32)


def merge_branches(x2, attn2, ssd2, proj2, wa, ws, wo, tm):
    T, D = x2.shape
    full = lambda a: pl.BlockSpec(a.shape, lambda i: (0, 0))
    return pl.pallas_call(
        _merge_kernel,
        grid=(T // tm,),
        in_specs=[
            pl.BlockSpec((tm, D), lambda i: (i, 0)),
            pl.BlockSpec((tm, ATTN_WIDTH), lambda i: (i, 0)),
            pl.BlockSpec((tm, SSD_INNER), lambda i: (i, 0)),
            pl.BlockSpec((tm, D), lambda i: (i, COL_GA // D)),
            pl.BlockSpec((tm, D), lambda i: (i, COL_GS // D)),
            full(wa), full(ws), full(wo),
        ],
        out_specs=pl.BlockSpec((tm, D), lambda i: (i, 0)),
        out_shape=jax.ShapeDtypeStruct((T, D), F32),
        compiler_params=_cparams(("parallel",)),
        name="merge",
    )(x2, attn2, ssd2, proj2, proj2, wa, ws, wo)


def _topk_rows(s, idx, k):
    n = s.shape[0]
    row = lax.broadcasted_iota(I32, s.shape, 0)
    vals, rows, picked = [], [], []
    for _ in range(k):
        m = jnp.max(s, axis=0, keepdims=True)
        first = jnp.min(jnp.where(s == m, row, n), axis=0, keepdims=True)
        hit = row == first
        vals.append(m)
        rows.append(first)
        if idx is not None:
            picked.append(jnp.max(jnp.where(hit, idx, -1), axis=0, keepdims=True))
        s = jnp.where(hit, -jnp.inf, s)
    cat = lambda xs: jnp.concatenate(xs, axis=0)
    return cat(vals), cat(rows), (cat(picked) if idx is not None else None)


def _peer_topk_kernel(q_ref, k1_ref, k2_ref, eidx_ref, gw_ref, *, tt):
    half = PEER_QDIM // 2
    e_rows, g_rows = [], []
    for h in range(PEER_HEADS):
        qa = q_ref[:, h * PEER_QDIM:h * PEER_QDIM + half].astype(BF16)
        qb = q_ref[:, h * PEER_QDIM + half:(h + 1) * PEER_QDIM].astype(BF16)
        s1 = lax.dot_general(k1_ref[h], qa, NT_DIMS, preferred_element_type=F32)
        s2 = lax.dot_general(k2_ref[h], qb, NT_DIMS, preferred_element_type=F32)
        v1, i1, _ = _topk_rows(s1, None, PEER_TOPK)
        v2, i2, _ = _topk_rows(s2, None, PEER_TOPK)
        sub = lax.broadcasted_iota(I32, (SUBLANES, tt), 0)
        cand_parts = [v1[0:1, :] + v2]
        cidx_parts = [i1[0:1, :] * PEER_NKEYS + i2]
        for a in range(1, SUBLANES):
            keep = sub < PEER_TOPK // (a + 1)
            cand_parts.append(jnp.where(keep, v1[a:a + 1, :] + v2[0:SUBLANES, :], -jnp.inf))
            cidx_parts.append(i1[a:a + 1, :] * PEER_NKEYS + i2[0:SUBLANES, :])
        cand_parts.append(v1[SUBLANES:, :] + v2[0:1, :])
        cidx_parts.append(i1[SUBLANES:, :] * PEER_NKEYS + i2[0:1, :])
        cand = jnp.concatenate(cand_parts, axis=0)
        cidx = jnp.concatenate(cidx_parts, axis=0)
        sv, _, ex = _topk_rows(cand, cidx, PEER_TOPK)
        e = jnp.exp(sv - sv[0:1, :])
        g_rows.append(e / jnp.sum(e, axis=0, keepdims=True))
        e_rows.append(ex)
    eidx_ref[...] = jnp.concatenate(e_rows, axis=0).T
    gw = jnp.concatenate(g_rows, axis=0)
    for part in range(tt // LANES):
        gw_ref[part] = gw[:, part * LANES:(part + 1) * LANES]


def peer_topk(q, keys1, keys2, tt):
    T = q.shape[0]
    full3 = lambda a: pl.BlockSpec(a.shape, lambda i: (0, 0, 0))
    return pl.pallas_call(
        functools.partial(_peer_topk_kernel, tt=tt),
        grid=(T // tt,),
        in_specs=[pl.BlockSpec((tt, q.shape[1]), lambda i: (i, 0)), full3(keys1), full3(keys2)],
        out_specs=[
            pl.BlockSpec((tt, PEER_SLOTS), lambda i: (i, 0)),
            pl.BlockSpec((tt // LANES, PEER_SLOTS, LANES), lambda i: (i, 0, 0)),
        ],
        out_shape=[
            jax.ShapeDtypeStruct((T, PEER_SLOTS), I32),
            jax.ShapeDtypeStruct((T // LANES, PEER_SLOTS, LANES), F32),
        ],
        compiler_params=_cparams(("parallel",)),
        name="peer_topk",
    )(q, keys1, keys2)


def pack_table(tab):
    half = tab.shape[1] // 2
    bits = lax.bitcast_convert_type(tab.astype(BF16), jnp.uint16).astype(jnp.uint32)
    return lax.bitcast_convert_type((bits[:, :half] << 16) | bits[:, half:], I32)


def sc_gather_rows(table, idx, win=64):
    V, W = table.shape
    N = idx.shape[0]
    info = plsc.get_sparse_core_info()
    n_cores, n_sub = info.num_cores, info.num_subcores
    workers = n_cores * n_sub
    per_w = N // workers
    steps = per_w // win
    assert steps * win * workers == N and steps % 2 == 0
    mesh = plsc.VectorSubcoreMesh(core_axis_name="c", subcore_axis_name="s")
    dma = pltpu.SemaphoreType.DMA

    @functools.partial(
        pl.kernel, mesh=mesh,
        out_type=jax.ShapeDtypeStruct((N, W), table.dtype),
        scratch_types=[
            pltpu.VMEM((steps, win), I32),
            pltpu.VMEM((win, W), table.dtype),
            pltpu.VMEM((win, W), table.dtype),
            dma, dma, dma, dma,
        ],
    )
    def gather_kernel(table_hbm, idx_hbm, out_hbm, idx_v, rows0, rows1, g0, g1, w0, w1):
        wid = lax.axis_index("s") * n_cores + lax.axis_index("c")
        row0 = wid * steps
        pltpu.sync_copy(idx_hbm.at[pl.ds(row0, steps)], idx_v)
        slots = ((rows0, g0, w0), (rows1, g1, w1))

        def gather(i, slot):
            rows, gsem, _ = slots[slot]
            return pltpu.make_async_copy(table_hbm.at[idx_v.at[i]], rows, gsem)

        def write(i, slot):
            rows, _, wsem = slots[slot]
            off = pl.multiple_of((row0 + i) * win, win)
            return pltpu.make_async_copy(rows, out_hbm.at[pl.ds(off, win)], wsem)

        gather(0, 0).start()

        @pl.loop(0, steps, step=2)
        def _(i):
            @pl.when(i > 0)
            def _():
                write(i - 1, 1).wait()

            gather(i + 1, 1).start()
            gather(i, 0).wait()
            write(i, 0).start()
            write(i, 0).wait()

            @pl.when(i + 2 < steps)
            def _():
                gather(i + 2, 0).start()

            gather(i + 1, 1).wait()
            write(i + 1, 1).start()

        write(steps - 1, 1).wait()

    return gather_kernel(table, idx.reshape(N // win, win))


def _unpack_words(w):
    u = pltpu.bitcast(w, jnp.uint32)
    hi = pltpu.bitcast(u & jnp.uint32(0xFFFF0000), F32)
    lo = pltpu.bitcast(u << 16, F32)
    return hi, lo


def _peer_expert_kernel(x_ref, g_ref, ug_ref, vg_ref, gw_ref, o_ref, *, tt):
    i = pl.program_id(0)
    half = D_MODEL // 2
    x1 = x_ref[...]
    xn = x1 * lax.rsqrt(jnp.mean(x1 * x1, axis=-1, keepdims=True) + EPS) * g_ref[...]
    lane = lax.broadcasted_iota(I32, (PEER_SLOTS, LANES), 1)
    off = (i % (LANES // tt)) * tt
    act = jnp.zeros((PEER_SLOTS, LANES), F32)
    for t in range(tt):
        hi, lo = _unpack_words(ug_ref[t * PEER_SLOTS:(t + 1) * PEER_SLOTS, :])
        prod = hi * xn[t:t + 1, :half] + lo * xn[t:t + 1, half:]
        fold = prod[:, 0:LANES]
        for c in range(1, half // LANES):
            fold = fold + prod[:, c * LANES:(c + 1) * LANES]
        col = jnp.sum(fold, axis=-1, keepdims=True)
        act = jnp.where(lane == off + t, col, act)
    gelu = 0.5 * act * (1.0 + lax.erf(act * (2.0 ** -0.5)))
    hact = gelu * gw_ref[...]
    for t in range(tt):
        hcol = jnp.sum(jnp.where(lane == off + t, hact, 0.0), axis=-1, keepdims=True)
        hi, lo = _unpack_words(vg_ref[t * PEER_SLOTS:(t + 1) * PEER_SLOTS, :])
        o_ref[t:t + 1, :half] = x1[t:t + 1, :half] + jnp.sum(hcol * hi, axis=0, keepdims=True)
        o_ref[t:t + 1, half:] = x1[t:t + 1, half:] + jnp.sum(hcol * lo, axis=0, keepdims=True)


def peer_experts(x1, g2, ug, vg, gw, tt=16):
    T, D = x1.shape
    W = ug.shape[1]
    per = LANES // tt
    return pl.pallas_call(
        functools.partial(_peer_expert_kernel, tt=tt),
        grid=(T // tt,),
        in_specs=[
            pl.BlockSpec((tt, D), lambda i: (i, 0)),
            pl.BlockSpec((1, D), lambda i: (0, 0)),
            pl.BlockSpec((tt * PEER_SLOTS, W), lambda i: (i, 0)),
            pl.BlockSpec((tt * PEER_SLOTS, W), lambda i: (i, 0)),
            pl.BlockSpec((None, PEER_SLOTS, LANES), lambda i: (i // per, 0, 0)),
        ],
        out_specs=pl.BlockSpec((tt, D), lambda i: (i, 0)),
        out_shape=jax.ShapeDtypeStruct((T, D), F32),
        compiler_params=_cparams(("parallel",)),
        name="peer_experts",
    )(x1, g2, ug, vg, gw)


def peer_ffn_residual(x1, norm2_g, wq, keys1, keys2, u_packed, v_packed):
    T = x1.shape[0]
    g2 = norm2_g.astype(F32)[None, :]
    q = norm_matmul(x1, g2, wq, F32, min(1024, T), 512)
    eidx, gw = peer_topk(q, keys1, keys2, min(256, T))
    flat = eidx.reshape(-1)
    ug = sc_gather_rows(u_packed, flat)
    vg = sc_gather_rows(v_packed, flat)
    return peer_experts(x1, g2, ug, vg, gw)


def kernel(x, norm1_g, w_in, q_norm_g, k_norm_g, conv_w, conv_b, dt_bias, a_log, d_skip, ssd_norm_g,
           w_attn_o, w_ssd_o, w_out, norm2_g, w_peer_q, peer_keys1, peer_keys2, peer_u, peer_v):
    B, S, D = x.shape
    xs = [x[b] for b in range(B)]
    for l in range(norm1_g.shape[0]):
        w = w_in[l]
        dt0 = COL_GA
        w_main = jnp.concatenate([w[:, :dt0], w[:, dt0 + SSD_HEADS:]], axis=1).astype(BF16)
        w_dt = jnp.zeros((D, LANES), BF16).at[:, :SSD_HEADS].set(w[:, dt0:dt0 + SSD_HEADS].astype(BF16))
        g1 = norm1_g[l].astype(F32)[None, :]
        wa, ws, wo = w_attn_o[l].astype(BF16), w_ssd_o[l].astype(BF16), w_out[l].astype(BF16)
        wq = w_peer_q[l].astype(BF16)
        k1, k2 = peer_keys1[l].astype(BF16), peer_keys2[l].astype(BF16)
        up, vp = pack_table(peer_u[l]), pack_table(peer_v[l])
        tm = min(1024, S)
        for b in range(B):
            x2 = xs[b]
            proj = norm_matmul(x2, g1, w_main, BF16, tm, 512)
            dtraw = norm_matmul(x2, g1, w_dt, F32, tm, LANES)
            proj3 = proj[None]
            attn = moba_attention(proj3, q_norm_g[l], k_norm_g[l])
            yssd = ssd_mixer(proj3, dtraw[None], conv_w[l], conv_b[l], dt_bias[l],
                             a_log[l], d_skip[l], ssd_norm_g[l])
            x1 = merge_branches(x2, attn[0], yssd[0], proj, wa, ws, wo, min(512, S))
            xs[b] = peer_ffn_residual(x1, norm2_g[l], wq, k1, k2, up, vp)
    return jnp.stack(xs, axis=0)
```

```python
import functools

import jax
import jax.numpy as jnp
import numpy as np
from jax import lax
from jax.experimental import pallas as pl
from jax.experimental.pallas import tpu as pltpu
from jax.experimental.pallas import tpu_sc as plsc

F32 = jnp.float32
BF16 = jnp.bfloat16
I32 = jnp.int32

EPS = 1e-6
D_MODEL = 1024
N_HEADS = 16
HEAD_DIM = 64
ATTN_WIDTH = N_HEADS * HEAD_DIM
MOBA_BLOCK = 256
MOBA_TOPK = 3
MOBA_KEY_GROUP = 4
MOBA_Q_TILE = 1024
SSD_INNER = 2048
SSD_HEADS = 32
SSD_GROUPS = 8
SSD_STATE = 128
SSD_CONV = 4
SSD_CHUNK = 256
SSD_GN = SSD_GROUPS * SSD_STATE
SSD_GROUP_W = SSD_INNER // SSD_GROUPS
SSD_GROUPS_PER_STEP = 2
PEER_HEADS = 8
PEER_NKEYS = 128
PEER_QDIM = 256
PEER_TOPK = 16
PEER_SLOTS = PEER_HEADS * PEER_TOPK

LANES = 128
SUBLANES = 8
VMEM_LIMIT = 56 * 1024 * 1024
MASK_BIG = 1e30

COL_Q, COL_K, COL_V = 0, ATTN_WIDTH, 2 * ATTN_WIDTH
COL_Z = 3 * ATTN_WIDTH
COL_X = COL_Z + SSD_INNER
COL_B = COL_X + SSD_INNER
COL_C = COL_B + SSD_GN
COL_GA = COL_C + SSD_GN
COL_GS = COL_GA + D_MODEL
PROJ_COLS = COL_GS + D_MODEL

NT_DIMS = (((1,), (1,)), ((), ()))


def _cparams(sem):
    return pltpu.CompilerParams(dimension_semantics=sem, vmem_limit_bytes=VMEM_LIMIT)


def _norm_matmul_kernel(x_ref, g_ref, w_ref, o_ref, h_ref):
    @pl.when(pl.program_id(1) == 0)
    def _():
        x = x_ref[...]
        y = x * lax.rsqrt(jnp.mean(x * x, axis=-1, keepdims=True) + EPS)
        h_ref[...] = (y * g_ref[...]).astype(h_ref.dtype)

    o_ref[...] = jnp.dot(h_ref[...], w_ref[...], preferred_element_type=F32).astype(o_ref.dtype)


def norm_matmul(x, g, w, out_dtype, tm, tn):
    T, K = x.shape
    N = w.shape[1]
    return pl.pallas_call(
        _norm_matmul_kernel,
        grid=(T // tm, N // tn),
        in_specs=[
            pl.BlockSpec((tm, K), lambda i, j: (i, 0)),
            pl.BlockSpec((1, K), lambda i, j: (0, 0)),
            pl.BlockSpec((K, tn), lambda i, j: (0, j)),
        ],
        out_specs=pl.BlockSpec((tm, tn), lambda i, j: (i, j)),
        out_shape=jax.ShapeDtypeStruct((T, N), out_dtype),
        scratch_shapes=[pltpu.VMEM((tm, K), BF16)],
        compiler_params=_cparams(("parallel", "arbitrary")),
        name="norm_matmul",
    )(x, g, w)


def _split3(v):
    hi = v.astype(BF16).astype(F32)
    r1 = v - hi
    mid = r1.astype(BF16).astype(F32)
    return hi, mid, r1 - mid


def _head_pair_norm(x, g):
    lane = lax.broadcasted_iota(I32, x.shape, 1)
    low = lane < HEAD_DIM
    x2 = x * x
    ss_a = jnp.sum(jnp.where(low, x2, 0.0), axis=-1, keepdims=True)
    ss_b = jnp.sum(jnp.where(low, 0.0, x2), axis=-1, keepdims=True)
    inv = jnp.where(low, lax.rsqrt(ss_a / HEAD_DIM + EPS), lax.rsqrt(ss_b / HEAD_DIM + EPS))
    return x * inv * g


def _kprep_kernel(k_ref, v_ref, g_ref, sl_ref, kaug_ref, kmean_ref, vt_ref, *, tq, tk):
    s_idx = pl.program_id(2)
    for grp in range(tq // tk):
        vblk = v_ref[grp * tk:(grp + 1) * tk, :].astype(F32)
        vt_ref[grp] = vblk.T.astype(BF16)
    kn = _head_pair_norm(k_ref[...].astype(F32), g_ref[...])
    nb = tq // MOBA_BLOCK
    km = jnp.mean(kn.reshape(nb, MOBA_BLOCK, LANES), axis=1)
    lane = lax.broadcasted_iota(I32, (tq, LANES), 1)
    row = lax.broadcasted_iota(I32, (tq, LANES), 0) + s_idx * tq
    blk = row // MOBA_BLOCK
    pos = row.astype(F32)
    lane_m = lax.broadcasted_iota(I32, (nb, LANES), 1)
    heads = ((kn, km), (pltpu.roll(kn, HEAD_DIM, axis=1), pltpu.roll(km, HEAD_DIM, axis=1)))
    for hh, (kk, kmm) in enumerate(heads):
        hi, mid, lo = _split3(sl_ref[hh:hh + 1, :] * pos)
        aug = jnp.where(lane < HEAD_DIM, kk, 0.0)
        aug = jnp.where((lane >= 64) & (lane < 96), (lane - 64 == blk).astype(F32), aug)
        aug = jnp.where(lane == 96, hi, aug)
        aug = jnp.where(lane == 97, mid, aug)
        aug = jnp.where(lane == 98, lo, aug)
        aug = jnp.where((lane >= 99) & (lane < 102), 1.0, aug)
        kaug_ref[hh] = aug.astype(BF16)
        kmean_ref[hh] = jnp.where(lane_m < HEAD_DIM, kmm, 0.0)


def _qprep_kernel(q_ref, g_ref, sl_ref, kmean_ref, qaug_ref, *, tq, nblk):
    s_idx = pl.program_id(2)
    qn = _head_pair_norm(q_ref[...].astype(F32), g_ref[...])
    lane = lax.broadcasted_iota(I32, (tq, LANES), 1)
    t = (lax.broadcasted_iota(I32, (tq, LANES), 0) + s_idx * tq).astype(F32)
    jrow = lax.broadcasted_iota(I32, (32, tq), 0)
    own = (lax.broadcasted_iota(I32, (32, tq), 1) + s_idx * tq) // MOBA_BLOCK
    heads = (qn, pltpu.roll(qn, HEAD_DIM, axis=1))
    for hh, qh in enumerate(heads):
        qq = jnp.where(lane < HEAD_DIM, qh, 0.0)
        km_rows = jnp.concatenate([kmean_ref[hh], jnp.zeros((32 - nblk, LANES), F32)], axis=0) \
            if nblk < 32 else kmean_ref[hh]
        g = lax.dot_general(km_rows, qq, NT_DIMS, precision=lax.Precision.HIGHEST,
                            preferred_element_type=F32)
        g = jnp.where(jrow < own, g, -jnp.inf)
        allowed = jrow == own
        for r in range(MOBA_TOPK):
            m = jnp.max(g, axis=0, keepdims=True)
            first = jnp.min(jnp.where(g == m, jrow, 1 << 20), axis=0, keepdims=True)
            hit = jrow == first
            allowed = allowed | (hit & (own > r))
            g = jnp.where(hit, -jnp.inf, g)
        mask_t = jnp.where(allowed, 0.0, -MASK_BIG)
        mask = jnp.concatenate([jnp.zeros((64, tq), F32), mask_t, jnp.zeros((32, tq), F32)], axis=0).T
        hi, mid, lo = _split3(-sl_ref[hh:hh + 1, :] * t)
        aug = qq * (HEAD_DIM ** -0.5)
        aug = jnp.where((lane >= 64) & (lane < 96), mask, aug)
        aug = jnp.where((lane >= 96) & (lane < 99), 1.0, aug)
        aug = jnp.where(lane == 99, hi, aug)
        aug = jnp.where(lane == 100, mid, aug)
        aug = jnp.where(lane == 101, lo, aug)
        qaug_ref[hh] = aug.astype(BF16)


def _attn_kernel(q_ref, k_ref, vt_ref, o_ref, acc_ref, m_ref, *, tq, tk):
    i = pl.program_id(2)
    vrow = lax.broadcasted_iota(I32, (LANES, tk), 0)
    low = vrow < HEAD_DIM
    last = (i * tq) // tk

    def group(g, diag):
        start = pl.multiple_of(g * tk, tk)
        vt = vt_ref[g]
        one = jnp.ones_like(vt)
        if diag:
            kpos = lax.broadcasted_iota(I32, (tk, tq), 0) + g * tk
            qpos = lax.broadcasted_iota(I32, (tk, tq), 1) + i * tq
            causal = kpos <= qpos
        for hh in range(2):
            kj = k_ref[hh, pl.ds(start, tk), :]
            s = lax.dot_general(kj, q_ref[hh], NT_DIMS, preferred_element_type=F32)
            if diag:
                s = jnp.where(causal, s, -MASK_BIG)
            m_old = m_ref[hh]
            m_new = jnp.maximum(m_old, jnp.max(s, axis=0, keepdims=True))
            alpha = jnp.exp(m_old - m_new)
            p = jnp.exp(s - m_new).astype(BF16)
            vaug = jnp.where(low, vt, one) if hh == 0 else jnp.where(low, one, vt)
            acc_ref[hh] = alpha * acc_ref[hh] + jnp.dot(vaug, p, preferred_element_type=F32)
            m_ref[hh] = m_new

    acc_ref[...] = jnp.zeros_like(acc_ref)
    m_ref[...] = jnp.full_like(m_ref, -MASK_BIG)
    group(last, True)

    def body(g, carry):
        group(g, False)
        return carry

    lax.fori_loop(0, last, body, 0)
    a = acc_ref[0]
    b = acc_ref[1]
    low_q = lax.broadcasted_iota(I32, (LANES, tq), 0) < HEAD_DIM
    out = jnp.where(low_q, a / a[HEAD_DIM:HEAD_DIM + 1, :], b / b[0:1, :])
    o_ref[...] = out.T.astype(o_ref.dtype)


def moba_attention(proj3, q_norm_g, k_norm_g):
    B, S, _ = proj3.shape
    nblk = S // MOBA_BLOCK
    assert S % MOBA_BLOCK == 0 and nblk <= 32
    HP = N_HEADS // 2
    tq = min(2048, S)
    slopes = jnp.exp2(-8.0 * jnp.arange(1, N_HEADS + 1, dtype=F32) / N_HEADS)
    sl = jnp.zeros((HP, SUBLANES, LANES), F32)
    sl = sl.at[:, 0, :].set(slopes[0::2, None]).at[:, 1, :].set(slopes[1::2, None])
    gq = jnp.tile(q_norm_g.astype(F32), 2)[None, :]
    gk = jnp.tile(k_norm_g.astype(F32), 2)[None, :]
    qb, kb, vb = COL_Q // LANES, COL_K // LANES, COL_V // LANES
    grid = (B, HP, S // tq)
    sem3 = ("parallel", "parallel", "parallel")

    nbt = tq // MOBA_BLOCK
    tk = MOBA_KEY_GROUP * MOBA_BLOCK
    assert tq % tk == 0
    kaug, kmean, vt = pl.pallas_call(
        functools.partial(_kprep_kernel, tq=tq, tk=tk),
        grid=grid,
        in_specs=[
            pl.BlockSpec((None, tq, LANES), lambda b, p, s: (b, s, kb + p)),
            pl.BlockSpec((None, tq, LANES), lambda b, p, s: (b, s, vb + p)),
            pl.BlockSpec((1, LANES), lambda b, p, s: (0, 0)),
            pl.BlockSpec((None, SUBLANES, LANES), lambda b, p, s: (p, 0, 0)),
        ],
        out_specs=[
            pl.BlockSpec((None, 2, tq, LANES), lambda b, p, s: (b, p, s, 0)),
            pl.BlockSpec((None, 2, nbt, LANES), lambda b, p, s: (b, p, s, 0)),
            pl.BlockSpec((None, None, tq // tk, LANES, tk), lambda b, p, s: (b, p, s, 0, 0)),
        ],
        out_shape=[
            jax.ShapeDtypeStruct((B, N_HEADS, S, LANES), BF16),
            jax.ShapeDtypeStruct((B, N_HEADS, nblk, LANES), F32),
            jax.ShapeDtypeStruct((B, HP, S // tk, LANES, tk), BF16),
        ],
        compiler_params=_cparams(sem3),
        name="moba_kprep",
    )(proj3, proj3, gk, sl)

    qaug = pl.pallas_call(
        functools.partial(_qprep_kernel, tq=tq, nblk=nblk),
        grid=grid,
        in_specs=[
            pl.BlockSpec((None, tq, LANES), lambda b, p, s: (b, s, qb + p)),
            pl.BlockSpec((1, LANES), lambda b, p, s: (0, 0)),
            pl.BlockSpec((None, SUBLANES, LANES), lambda b, p, s: (p, 0, 0)),
            pl.BlockSpec((None, 2, nblk, LANES), lambda b, p, s: (b, p, 0, 0)),
        ],
        out_specs=pl.BlockSpec((None, 2, tq, LANES), lambda b, p, s: (b, p, s, 0)),
        out_shape=jax.ShapeDtypeStruct((B, N_HEADS, S, LANES), BF16),
        compiler_params=_cparams(sem3),
        name="moba_qprep",
    )(proj3, gq, sl, kmean)

    ta = min(MOBA_Q_TILE, S)
    assert tk % ta == 0
    return pl.pallas_call(
        functools.partial(_attn_kernel, tq=ta, tk=tk),
        grid=(B, HP, S // ta),
        in_specs=[
            pl.BlockSpec((None, 2, ta, LANES), lambda b, p, i: (b, p, i, 0)),
            pl.BlockSpec((None, 2, S, LANES), lambda b, p, i: (b, p, 0, 0)),
            pl.BlockSpec((None, None, S // tk, LANES, tk), lambda b, p, i: (b, p, 0, 0, 0)),
        ],
        out_specs=pl.BlockSpec((None, ta, LANES), lambda b, p, i: (b, i, p)),
        out_shape=jax.ShapeDtypeStruct((B, S, ATTN_WIDTH), BF16),
        scratch_shapes=[pltpu.VMEM((2, LANES, ta), F32), pltpu.VMEM((2, 1, ta), F32)],
        compiler_params=_cparams(("parallel", "parallel", "arbitrary")),
        name="moba_attn",
    )(qaug, kaug, vt)


def _silu(x):
    return x * (1.0 / (1.0 + jnp.exp(-x)))


def _conv_silu(u_ref, halo_ref, w_ref, b_ref, first):
    u = u_ref[...].astype(F32)
    halo = jnp.where(first, 0.0, halo_ref[...].astype(F32))
    ext = jnp.concatenate([halo, u], axis=0)
    w = w_ref[...]
    out = b_ref[...] + w[3:4, :] * u
    for back in range(1, SSD_CONV):
        out = out + w[3 - back:4 - back, :] * pltpu.roll(ext, back, axis=0)[SUBLANES:, :]
    return _silu(out)


def _ssd_group(xs, bm, cm, z, dt, rg, a, d, ng, state_ref):
    L, W = xs.shape
    hp = lax.Precision.HIGHEST
    dtx = jnp.dot(dt, rg, precision=hp, preferred_element_type=F32)
    ax = dtx * a
    rr = lax.broadcasted_iota(I32, (L, L), 0)
    cc = lax.broadcasted_iota(I32, (L, L), 1)
    causal = cc <= rr
    acum = jnp.dot(causal.astype(F32), ax, precision=hp, preferred_element_type=F32)
    acum_t = acum.T
    a_last = acum[L - 1:L, :]
    xdt = xs * dtx
    cmb = cm.astype(BF16)
    cb = lax.dot_general(cmb, bm.astype(BF16), NT_DIMS, preferred_element_type=F32)
    lane = lax.broadcasted_iota(I32, (L, W), 1)
    y = jnp.zeros((L, W), F32)
    for r in range(W // 64):
        seg = acum[:, 64 * r:64 * r + 1] - acum_t[64 * r:64 * r + 1, :]
        lmat = jnp.exp(jnp.where(causal, seg, -jnp.inf))
        xr = jnp.where((lane >= 64 * r) & (lane < 64 * r + 64), xdt, 0.0).astype(BF16)
        y = y + jnp.dot((cb * lmat).astype(BF16), xr, preferred_element_type=F32)
    state = state_ref[...]
    y = y + jnp.dot(cmb, state.astype(BF16), preferred_element_type=F32) * jnp.exp(acum)
    wgt = (xdt * jnp.exp(a_last - acum)).astype(BF16)
    state_ref[...] = state * jnp.exp(a_last) + jnp.dot(bm.T.astype(BF16), wgt, preferred_element_type=F32)
    y = y + d * xs
    y = y * _silu(z)
    y = y * lax.rsqrt(jnp.mean(y * y, axis=-1, keepdims=True) + EPS)
    return y * ng


def _ssd_kernel(x_ref, xh_ref, b_ref, bh_ref, c_ref, ch_ref, z_ref, dt_ref,
                wx_ref, wb_ref, wc_ref, bx_ref, bb_ref, bc_ref, dtb_ref, rg_ref,
                a_ref, d_ref, ng_ref, y_ref, state_ref):
    first = pl.program_id(2) == 0
    W, N = SSD_GROUP_W, SSD_STATE

    @pl.when(first)
    def _():
        state_ref[...] = jnp.zeros_like(state_ref)

    xs = _conv_silu(x_ref, xh_ref, wx_ref, bx_ref, first)
    bm = _conv_silu(b_ref, bh_ref, wb_ref, bb_ref, first)
    cm = _conv_silu(c_ref, ch_ref, wc_ref, bc_ref, first)
    dt = jax.nn.softplus(dt_ref[...] + dtb_ref[...])
    z = z_ref[...].astype(F32)
    for gi in range(SSD_GROUPS_PER_STEP):
        cw, cn = slice(gi * W, (gi + 1) * W), slice(gi * N, (gi + 1) * N)
        y = _ssd_group(xs[:, cw], bm[:, cn], cm[:, cn], z[:, cw], dt, rg_ref[gi],
                       a_ref[:, cw], d_ref[:, cw], ng_ref[:, cw], state_ref.at[gi])
        y_ref[:, cw] = y.astype(y_ref.dtype)


def ssd_mixer(proj3, dtraw3, conv_w, conv_b, dt_bias, a_log, d_skip, norm_g):
    B, S, _ = proj3.shape
    GP = SSD_GROUPS_PER_STEP
    L, W, N = SSD_CHUNK, SSD_GROUP_W * GP, SSD_STATE * GP
    assert S % L == 0 and SSD_GROUPS % GP == 0
    nc = S // L
    hb = L // SUBLANES
    xb, bb, cb_, zb = COL_X // W, COL_B // N, COL_C // N, COL_Z // W
    rep = SSD_INNER // SSD_HEADS
    a_exp = jnp.repeat(-jnp.exp(a_log.astype(F32)), rep)[None, :]
    d_exp = jnp.repeat(d_skip.astype(F32), rep)[None, :]
    ng = norm_g.astype(F32)[None, :]
    dtb = jnp.zeros((1, LANES), F32).at[0, :SSD_HEADS].set(dt_bias.astype(F32))
    head_of_chan = np.arange(SSD_INNER) // rep
    rg = (np.arange(LANES)[None, :, None]
          == head_of_chan.reshape(SSD_GROUPS, 1, SSD_GROUP_W)).astype(np.float32)
    cw = conv_w.astype(F32)
    cbias = conv_b.astype(F32)[None, :]
    cxo, cbo, cco = 0, SSD_INNER // N, (SSD_INNER + SSD_GN) // N

    def halo(col):
        return lambda b, g, c: (b, jnp.maximum(c * hb - 1, 0), col + g)

    return pl.pallas_call(
        _ssd_kernel,
        grid=(B, SSD_GROUPS // GP, nc),
        in_specs=[
            pl.BlockSpec((None, L, W), lambda b, g, c: (b, c, xb + g)),
            pl.BlockSpec((None, SUBLANES, W), halo(xb)),
            pl.BlockSpec((None, L, N), lambda b, g, c: (b, c, bb + g)),
            pl.BlockSpec((None, SUBLANES, N), halo(bb)),
            pl.BlockSpec((None, L, N), lambda b, g, c: (b, c, cb_ + g)),
            pl.BlockSpec((None, SUBLANES, N), halo(cb_)),
            pl.BlockSpec((None, L, W), lambda b, g, c: (b, c, zb + g)),
            pl.BlockSpec((None, L, LANES), lambda b, g, c: (b, c, 0)),
            pl.BlockSpec((SSD_CONV, W), lambda b, g, c: (0, cxo + g)),
            pl.BlockSpec((SSD_CONV, N), lambda b, g, c: (0, cbo + g)),
            pl.BlockSpec((SSD_CONV, N), lambda b, g, c: (0, cco + g)),
            pl.BlockSpec((1, W), lambda b, g, c: (0, cxo + g)),
            pl.BlockSpec((1, N), lambda b, g, c: (0, cbo + g)),
            pl.BlockSpec((1, N), lambda b, g, c: (0, cco + g)),
            pl.BlockSpec((1, LANES), lambda b, g, c: (0, 0)),
            pl.BlockSpec((GP, LANES, SSD_GROUP_W), lambda b, g, c: (g, 0, 0)),
            pl.BlockSpec((1, W), lambda b, g, c: (0, g)),
            pl.BlockSpec((1, W), lambda b, g, c: (0, g)),
            pl.BlockSpec((1, W), lambda b, g, c: (0, g)),
        ],
        out_specs=pl.BlockSpec((None, L, W), lambda b, g, c: (b, c, g)),
        out_shape=jax.ShapeDtypeStruct((B, S, SSD_INNER), BF16),
        scratch_shapes=[pltpu.VMEM((GP, SSD_STATE, SSD_GROUP_W), F32)],
        compiler_params=_cparams(("parallel", "parallel", "arbitrary")),
        name="ssd_scan",
    )(proj3, proj3, proj3, proj3, proj3, proj3, proj3, dtraw3,
      cw, cw, cw, cbias, cbias, cbias, dtb, jnp.asarray(rg), a_exp, d_exp, ng)


def _merge_kernel(x_ref, a_ref, s_ref, ga_ref, gs_ref, wa_ref, ws_ref, wo_ref, o_ref):
    ya = jnp.dot(a_ref[...], wa_ref[...], preferred_element_type=F32)
    ys = jnp.dot(s_ref[...], ws_ref[...], preferred_element_type=F32)
    mixed = jax.nn.sigmoid(ga_ref[...].astype(F32)) * ya + jax.nn.sigmoid(gs_ref[...].astype(F32)) * ys
    o_ref[...] = x_ref[...] + jnp.dot(mixed.astype(BF16), wo_ref[...], preferred_element_type=F32)


def merge_branches(x2, attn2, ssd2, proj2, wa, ws, wo, tm):
    T, D = x2.shape
    full = lambda a: pl.BlockSpec(a.shape, lambda i: (0, 0))
    return pl.pallas_call(
        _merge_kernel,
        grid=(T // tm,),
        in_specs=[
            pl.BlockSpec((tm, D), lambda i: (i, 0)),
            pl.BlockSpec((tm, ATTN_WIDTH), lambda i: (i, 0)),
            pl.BlockSpec((tm, SSD_INNER), lambda i: (i, 0)),
            pl.BlockSpec((tm, D), lambda i: (i, COL_GA // D)),
            pl.BlockSpec((tm, D), lambda i: (i, COL_GS // D)),
            full(wa), full(ws), full(wo),
        ],
        out_specs=pl.BlockSpec((tm, D), lambda i: (i, 0)),
        out_shape=jax.ShapeDtypeStruct((T, D), F32),
        compiler_params=_cparams(("parallel",)),
        name="merge",
    )(x2, attn2, ssd2, proj2, proj2, wa, ws, wo)


def _topk_rows(s, idx, k):
    n = s.shape[0]
    row = lax.broadcasted_iota(I32, s.shape, 0)
    vals, rows, picked = [], [], []
    for _ in range(k):
        m = jnp.max(s, axis=0, keepdims=True)
        first = jnp.min(jnp.where(s == m, row, n), axis=0, keepdims=True)
        hit = row == first
        vals.append(m)
        rows.append(first)
        if idx is not None:
            picked.append(jnp.max(jnp.where(hit, idx, -1), axis=0, keepdims=True))
        s = jnp.where(hit, -jnp.inf, s)
    cat = lambda xs: jnp.concatenate(xs, axis=0)
    return cat(vals), cat(rows), (cat(picked) if idx is not None else None)


def _peer_topk_kernel(q_ref, k1_ref, k2_ref, eidx_ref, gw_ref, *, tt):
    half = PEER_QDIM // 2
    e_rows, g_rows = [], []
    for h in range(PEER_HEADS):
        qa = q_ref[:, h * PEER_QDIM:h * PEER_QDIM + half].astype(BF16)
        qb = q_ref[:, h * PEER_QDIM + half:(h + 1) * PEER_QDIM].astype(BF16)
        s1 = lax.dot_general(k1_ref[h], qa, NT_DIMS, preferred_element_type=F32)
        s2 = lax.dot_general(k2_ref[h], qb, NT_DIMS, preferred_element_type=F32)
        v1, i1, _ = _topk_rows(s1, None, PEER_TOPK)
        v2, i2, _ = _topk_rows(s2, None, PEER_TOPK)
        sub = lax.broadcasted_iota(I32, (SUBLANES, tt), 0)
        cand_parts = [v1[0:1, :] + v2]
        cidx_parts = [i1[0:1, :] * PEER_NKEYS + i2]
        for a in range(1, SUBLANES):
            keep = sub < PEER_TOPK // (a + 1)
            cand_parts.append(jnp.where(keep, v1[a:a + 1, :] + v2[0:SUBLANES, :], -jnp.inf))
            cidx_parts.append(i1[a:a + 1, :] * PEER_NKEYS + i2[0:SUBLANES, :])
        cand_parts.append(v1[SUBLANES:, :] + v2[0:1, :])
        cidx_parts.append(i1[SUBLANES:, :] * PEER_NKEYS + i2[0:1, :])
        cand = jnp.concatenate(cand_parts, axis=0)
        cidx = jnp.concatenate(cidx_parts, axis=0)
        sv, _, ex = _topk_rows(cand, cidx, PEER_TOPK)
        e = jnp.exp(sv - sv[0:1, :])
        g_rows.append(e / jnp.sum(e, axis=0, keepdims=True))
        e_rows.append(ex)
    eidx_ref[...] = jnp.concatenate(e_rows, axis=0).T
    gw = jnp.concatenate(g_rows, axis=0)
    for part in range(tt // LANES):
        gw_ref[part] = gw[:, part * LANES:(part + 1) * LANES]


def peer_topk(q, keys1, keys2, tt):
    T = q.shape[0]
    full3 = lambda a: pl.BlockSpec(a.shape, lambda i: (0, 0, 0))
    return pl.pallas_call(
        functools.partial(_peer_topk_kernel, tt=tt),
        grid=(T // tt,),
        in_specs=[pl.BlockSpec((tt, q.shape[1]), lambda i: (i, 0)), full3(keys1), full3(keys2)],
        out_specs=[
            pl.BlockSpec((tt, PEER_SLOTS), lambda i: (i, 0)),
            pl.BlockSpec((tt // LANES, PEER_SLOTS, LANES), lambda i: (i, 0, 0)),
        ],
        out_shape=[
            jax.ShapeDtypeStruct((T, PEER_SLOTS), I32),
            jax.ShapeDtypeStruct((T // LANES, PEER_SLOTS, LANES), F32),
        ],
        compiler_params=_cparams(("parallel",)),
        name="peer_topk",
    )(q, keys1, keys2)


def pack_table(tab):
    half = tab.shape[1] // 2
    bits = lax.bitcast_convert_type(tab.astype(BF16), jnp.uint16).astype(jnp.uint32)
    return lax.bitcast_convert_type((bits[:, :half] << 16) | bits[:, half:], I32)


def sc_gather_rows(table, idx, win=64):
    V, W = table.shape
    N = idx.shape[0]
    info = plsc.get_sparse_core_info()
    n_cores, n_sub = info.num_cores, info.num_subcores
    workers = n_cores * n_sub
    per_w = N // workers
    steps = per_w // win
    assert steps * win * workers == N and steps % 2 == 0
    mesh = plsc.VectorSubcoreMesh(core_axis_name="c", subcore_axis_name="s")
    dma = pltpu.SemaphoreType.DMA

    @functools.partial(
        pl.kernel, mesh=mesh,
        out_type=jax.ShapeDtypeStruct((N, W), table.dtype),
        scratch_types=[
            pltpu.VMEM((steps, win), I32),
            pltpu.VMEM((win, W), table.dtype),
            pltpu.VMEM((win, W), table.dtype),
            dma, dma, dma, dma,
        ],
    )
    def gather_kernel(table_hbm, idx_hbm, out_hbm, idx_v, rows0, rows1, g0, g1, w0, w1):
        wid = lax.axis_index("s") * n_cores + lax.axis_index("c")
        row0 = wid * steps
        pltpu.sync_copy(idx_hbm.at[pl.ds(row0, steps)], idx_v)
        slots = ((rows0, g0, w0), (rows1, g1, w1))

        def gather(i, slot):
            rows, gsem, _ = slots[slot]
            return pltpu.make_async_copy(table_hbm.at[idx_v.at[i]], rows, gsem)

        def write(i, slot):
            rows, _, wsem = slots[slot]
            off = pl.multiple_of((row0 + i) * win, win)
            return pltpu.make_async_copy(rows, out_hbm.at[pl.ds(off, win)], wsem)

        gather(0, 0).start()

        @pl.loop(0, steps, step=2)
        def _(i):
            @pl.when(i > 0)
            def _():
                write(i - 1, 1).wait()

            gather(i + 1, 1).start()
            gather(i, 0).wait()
            write(i, 0).start()
            write(i, 0).wait()

            @pl.when(i + 2 < steps)
            def _():
                gather(i + 2, 0).start()

            gather(i + 1, 1).wait()
            write(i + 1, 1).start()

        write(steps - 1, 1).wait()

    return gather_kernel(table, idx.reshape(N // win, win))


def _unpack_words(w):
    u = pltpu.bitcast(w, jnp.uint32)
    hi = pltpu.bitcast(u & jnp.uint32(0xFFFF0000), F32)
    lo = pltpu.bitcast(u << 16, F32)
    return hi, lo


def _peer_expert_kernel(x_ref, g_ref, ug_ref, vg_ref, gw_ref, o_ref, *, tt):
    i = pl.program_id(0)
    half = D_MODEL // 2
    x1 = x_ref[...]
    xn = x1 * lax.rsqrt(jnp.mean(x1 * x1, axis=-1, keepdims=True) + EPS) * g_ref[...]
    lane = lax.broadcasted_iota(I32, (PEER_SLOTS, LANES), 1)
    off = (i % (LANES // tt)) * tt
    act = jnp.zeros((PEER_SLOTS, LANES), F32)
    for t in range(tt):
        hi, lo = _unpack_words(ug_ref[t * PEER_SLOTS:(t + 1) * PEER_SLOTS, :])
        prod = hi * xn[t:t + 1, :half] + lo * xn[t:t + 1, half:]
        fold = prod[:, 0:LANES]
        for c in range(1, half // LANES):
            fold = fold + prod[:, c * LANES:(c + 1) * LANES]
        col = jnp.sum(fold, axis=-1, keepdims=True)
        act = jnp.where(lane == off + t, col, act)
    gelu = 0.5 * act * (1.0 + lax.erf(act * (2.0 ** -0.5)))
    hact = gelu * gw_ref[...]
    for t in range(tt):
        hcol = jnp.sum(jnp.where(lane == off + t, hact, 0.0), axis=-1, keepdims=True)
        hi, lo = _unpack_words(vg_ref[t * PEER_SLOTS:(t + 1) * PEER_SLOTS, :])
        o_ref[t:t + 1, :half] = x1[t:t + 1, :half] + jnp.sum(hcol * hi, axis=0, keepdims=True)
        o_ref[t:t + 1, half:] = x1[t:t + 1, half:] + jnp.sum(hcol * lo, axis=0, keepdims=True)


def peer_experts(x1, g2, ug, vg, gw, tt=16):
    T, D = x1.shape
    W = ug.shape[1]
    per = LANES // tt
    return pl.pallas_call(
        functools.partial(_peer_expert_kernel, tt=tt),
        grid=(T // tt,),
        in_specs=[
            pl.BlockSpec((tt, D), lambda i: (i, 0)),
            pl.BlockSpec((1, D), lambda i: (0, 0)),
            pl.BlockSpec((tt * PEER_SLOTS, W), lambda i: (i, 0)),
            pl.BlockSpec((tt * PEER_SLOTS, W), lambda i: (i, 0)),
            pl.BlockSpec((None, PEER_SLOTS, LANES), lambda i: (i // per, 0, 0)),
        ],
        out_specs=pl.BlockSpec((tt, D), lambda i: (i, 0)),
        out_shape=jax.ShapeDtypeStruct((T, D), F32),
        compiler_params=_cparams(("parallel",)),
        name="peer_experts",
    )(x1, g2, ug, vg, gw)


def peer_ffn_residual(x1, norm2_g, wq, keys1, keys2, u_packed, v_packed):
    T = x1.shape[0]
    g2 = norm2_g.astype(F32)[None, :]
    q = norm_matmul(x1, g2, wq, F32, min(1024, T), 1024)
    eidx, gw = peer_topk(q, keys1, keys2, min(256, T))
    flat = eidx.reshape(-1)
    ug = sc_gather_rows(u_packed, flat)
    vg = sc_gather_rows(v_packed, flat)
    return peer_experts(x1, g2, ug, vg, gw)


def kernel(x, norm1_g, w_in, q_norm_g, k_norm_g, conv_w, conv_b, dt_bias, a_log, d_skip, ssd_norm_g,
           w_attn_o, w_ssd_o, w_out, norm2_g, w_peer_q, peer_keys1, peer_keys2, peer_u, peer_v):
    B, S, D = x.shape
    xs = [x[b] for b in range(B)]
    for l in range(norm1_g.shape[0]):
        w = w_in[l]
        dt0 = COL_GA
        w_main = jnp.concatenate([w[:, :dt0], w[:, dt0 + SSD_HEADS:]], axis=1).astype(BF16)
        w_dt = jnp.zeros((D, LANES), BF16).at[:, :SSD_HEADS].set(w[:, dt0:dt0 + SSD_HEADS].astype(BF16))
        g1 = norm1_g[l].astype(F32)[None, :]
        wa, ws, wo = w_attn_o[l].astype(BF16), w_ssd_o[l].astype(BF16), w_out[l].astype(BF16)
        wq = w_peer_q[l].astype(BF16)
        k1, k2 = peer_keys1[l].astype(BF16), peer_keys2[l].astype(BF16)
        up, vp = pack_table(peer_u[l]), pack_table(peer_v[l])
        tm = min(1024, S)
        for b in range(B):
            x2 = xs[b]
            proj = norm_matmul(x2, g1, w_main, BF16, tm, 1024)
            dtraw = norm_matmul(x2, g1, w_dt, F32, tm, LANES)
            proj3 = proj[None]
            attn = moba_attention(proj3, q_norm_g[l], k_norm_g[l])
            yssd = ssd_mixer(proj3, dtraw[None], conv_w[l], conv_b[l], dt_bias[l],
                             a_log[l], d_skip[l], ssd_norm_g[l])
            x1 = merge_branches(x2, attn[0], yssd[0], proj, wa, ws, wo, min(512, S))
            xs[b] = peer_ffn_residual(x1, norm2_g[l], wq, k1, k2, up, vp)
    return jnp.stack(xs, axis=0)
```

```python
import functools

import jax
import jax.numpy as jnp
import numpy as np
from jax import lax
from jax.experimental import pallas as pl
from jax.experimental.pallas import tpu as pltpu
from jax.experimental.pallas import tpu_sc as plsc

F32 = jnp.float32
BF16 = jnp.bfloat16
I32 = jnp.int32

EPS = 1e-6
D_MODEL = 1024
N_HEADS = 16
HEAD_DIM = 64
ATTN_WIDTH = N_HEADS * HEAD_DIM
MOBA_BLOCK = 256
MOBA_TOPK = 3
MOBA_KEY_GROUP = 4
MOBA_Q_TILE = 1024
SSD_INNER = 2048
SSD_HEADS = 32
SSD_GROUPS = 8
SSD_STATE = 128
SSD_CONV = 4
SSD_CHUNK = 256
SSD_GN = SSD_GROUPS * SSD_STATE
SSD_GROUP_W = SSD_INNER // SSD_GROUPS
SSD_GROUPS_PER_STEP = 2
PEER_HEADS = 8
PEER_NKEYS = 128
PEER_QDIM = 256
PEER_TOPK = 16
PEER_SLOTS = PEER_HEADS * PEER_TOPK

LANES = 128
SUBLANES = 8
VMEM_LIMIT = 56 * 1024 * 1024
MASK_BIG = 1e30

COL_Q, COL_K, COL_V = 0, ATTN_WIDTH, 2 * ATTN_WIDTH
COL_Z = 3 * ATTN_WIDTH
COL_X = COL_Z + SSD_INNER
COL_B = COL_X + SSD_INNER
COL_C = COL_B + SSD_GN
COL_GA = COL_C + SSD_GN
COL_GS = COL_GA + D_MODEL
PROJ_COLS = COL_GS + D_MODEL

NT_DIMS = (((1,), (1,)), ((), ()))


def _cparams(sem):
    return pltpu.CompilerParams(dimension_semantics=sem, vmem_limit_bytes=VMEM_LIMIT)


def _norm_matmul_kernel(x_ref, g_ref, w_ref, o_ref, h_ref):
    @pl.when(pl.program_id(1) == 0)
    def _():
        x = x_ref[...]
        y = x * lax.rsqrt(jnp.mean(x * x, axis=-1, keepdims=True) + EPS)
        h_ref[...] = (y * g_ref[...]).astype(h_ref.dtype)

    o_ref[...] = jnp.dot(h_ref[...], w_ref[...], preferred_element_type=F32).astype(o_ref.dtype)


def norm_matmul(x, g, w, out_dtype, tm, tn):
    T, K = x.shape
    N = w.shape[1]
    return pl.pallas_call(
        _norm_matmul_kernel,
        grid=(T // tm, N // tn),
        in_specs=[
            pl.BlockSpec((tm, K), lambda i, j: (i, 0)),
            pl.BlockSpec((1, K), lambda i, j: (0, 0)),
            pl.BlockSpec((K, tn), lambda i, j: (0, j)),
        ],
        out_specs=pl.BlockSpec((tm, tn), lambda i, j: (i, j)),
        out_shape=jax.ShapeDtypeStruct((T, N), out_dtype),
        scratch_shapes=[pltpu.VMEM((tm, K), BF16)],
        compiler_params=_cparams(("parallel", "arbitrary")),
        name="norm_matmul",
    )(x, g, w)


def _split3(v):
    hi = v.astype(BF16).astype(F32)
    r1 = v - hi
    mid = r1.astype(BF16).astype(F32)
    return hi, mid, r1 - mid


def _head_pair_norm(x, g):
    lane = lax.broadcasted_iota(I32, x.shape, 1)
    low = lane < HEAD_DIM
    x2 = x * x
    ss_a = jnp.sum(jnp.where(low, x2, 0.0), axis=-1, keepdims=True)
    ss_b = jnp.sum(jnp.where(low, 0.0, x2), axis=-1, keepdims=True)
    inv = jnp.where(low, lax.rsqrt(ss_a / HEAD_DIM + EPS), lax.rsqrt(ss_b / HEAD_DIM + EPS))
    return x * inv * g


def _kprep_kernel(k_ref, v_ref, g_ref, sl_ref, kaug_ref, kmean_ref, vt_ref, *, tq, tk):
    s_idx = pl.program_id(2)
    for grp in range(tq // tk):
        vblk = v_ref[grp * tk:(grp + 1) * tk, :].astype(F32)
        vt_ref[grp] = vblk.T.astype(BF16)
    kn = _head_pair_norm(k_ref[...].astype(F32), g_ref[...])
    nb = tq // MOBA_BLOCK
    km = jnp.mean(kn.reshape(nb, MOBA_BLOCK, LANES), axis=1)
    lane = lax.broadcasted_iota(I32, (tq, LANES), 1)
    row = lax.broadcasted_iota(I32, (tq, LANES), 0) + s_idx * tq
    blk = row // MOBA_BLOCK
    pos = row.astype(F32)
    lane_m = lax.broadcasted_iota(I32, (nb, LANES), 1)
    heads = ((kn, km), (pltpu.roll(kn, HEAD_DIM, axis=1), pltpu.roll(km, HEAD_DIM, axis=1)))
    for hh, (kk, kmm) in enumerate(heads):
        hi, mid, lo = _split3(sl_ref[hh:hh + 1, :] * pos)
        aug = jnp.where(lane < HEAD_DIM, kk, 0.0)
        aug = jnp.where((lane >= 64) & (lane < 96), (lane - 64 == blk).astype(F32), aug)
        aug = jnp.where(lane == 96, hi, aug)
        aug = jnp.where(lane == 97, mid, aug)
        aug = jnp.where(lane == 98, lo, aug)
        aug = jnp.where((lane >= 99) & (lane < 102), 1.0, aug)
        kaug_ref[hh] = aug.astype(BF16)
        kmean_ref[hh] = jnp.where(lane_m < HEAD_DIM, kmm, 0.0)


def _qprep_kernel(q_ref, g_ref, sl_ref, kmean_ref, qaug_ref, *, tq, nblk):
    s_idx = pl.program_id(2)
    qn = _head_pair_norm(q_ref[...].astype(F32), g_ref[...])
    lane = lax.broadcasted_iota(I32, (tq, LANES), 1)
    t = (lax.broadcasted_iota(I32, (tq, LANES), 0) + s_idx * tq).astype(F32)
    jrow = lax.broadcasted_iota(I32, (32, tq), 0)
    own = (lax.broadcasted_iota(I32, (32, tq), 1) + s_idx * tq) // MOBA_BLOCK
    heads = (qn, pltpu.roll(qn, HEAD_DIM, axis=1))
    for hh, qh in enumerate(heads):
        qq = jnp.where(lane < HEAD_DIM, qh, 0.0)
        km_rows = jnp.concatenate([kmean_ref[hh], jnp.zeros((32 - nblk, LANES), F32)], axis=0) \
            if nblk < 32 else kmean_ref[hh]
        g = lax.dot_general(km_rows, qq, NT_DIMS, precision=lax.Precision.HIGHEST,
                            preferred_element_type=F32)
        g = jnp.where(jrow < own, g, -jnp.inf)
        allowed = jrow == own
        for r in range(MOBA_TOPK):
            m = jnp.max(g, axis=0, keepdims=True)
            first = jnp.min(jnp.where(g == m, jrow, 1 << 20), axis=0, keepdims=True)
            hit = jrow == first
            allowed = allowed | (hit & (own > r))
            g = jnp.where(hit, -jnp.inf, g)
        mask_t = jnp.where(allowed, 0.0, -MASK_BIG)
        mask = jnp.concatenate([jnp.zeros((64, tq), F32), mask_t, jnp.zeros((32, tq), F32)], axis=0).T
        hi, mid, lo = _split3(-sl_ref[hh:hh + 1, :] * t)
        aug = qq * (HEAD_DIM ** -0.5)
        aug = jnp.where((lane >= 64) & (lane < 96), mask, aug)
        aug = jnp.where((lane >= 96) & (lane < 99), 1.0, aug)
        aug = jnp.where(lane == 99, hi, aug)
        aug = jnp.where(lane == 100, mid, aug)
        aug = jnp.where(lane == 101, lo, aug)
        qaug_ref[hh] = aug.astype(BF16)


def _attn_kernel(q_ref, k_ref, vt_ref, o_ref, acc_ref, m_ref, *, tq, tk):
    i = pl.program_id(2)
    vrow = lax.broadcasted_iota(I32, (LANES, tk), 0)
    low = vrow < HEAD_DIM
    last = (i * tq) // tk

    def group(g, diag):
        start = pl.multiple_of(g * tk, tk)
        vt = vt_ref[g]
        one = jnp.ones_like(vt)
        if diag:
            kpos = lax.broadcasted_iota(I32, (tk, tq), 0) + g * tk
            qpos = lax.broadcasted_iota(I32, (tk, tq), 1) + i * tq
            causal = kpos <= qpos
        for hh in range(2):
            kj = k_ref[hh, pl.ds(start, tk), :]
            s = lax.dot_general(kj, q_ref[hh], NT_DIMS, preferred_element_type=F32)
            if diag:
                s = jnp.where(causal, s, -MASK_BIG)
            m_old = m_ref[hh]
            m_new = jnp.maximum(m_old, jnp.max(s, axis=0, keepdims=True))
            alpha = jnp.exp(m_old - m_new)
            p = jnp.exp(s - m_new).astype(BF16)
            vaug = jnp.where(low, vt, one) if hh == 0 else jnp.where(low, one, vt)
            acc_ref[hh] = alpha * acc_ref[hh] + jnp.dot(vaug, p, preferred_element_type=F32)
            m_ref[hh] = m_new

    acc_ref[...] = jnp.zeros_like(acc_ref)
    m_ref[...] = jnp.full_like(m_ref, -MASK_BIG)
    group(last, True)

    def body(g, carry):
        group(g, False)
        return carry

    lax.fori_loop(0, last, body, 0)
    a = acc_ref[0]
    b = acc_ref[1]
    low_q = lax.broadcasted_iota(I32, (LANES, tq), 0) < HEAD_DIM
    out = jnp.where(low_q, a / a[HEAD_DIM:HEAD_DIM + 1, :], b / b[0:1, :])
    o_ref[...] = out.T.astype(o_ref.dtype)


def moba_attention(proj3, q_norm_g, k_norm_g):
    B, S, _ = proj3.shape
    nblk = S // MOBA_BLOCK
    assert S % MOBA_BLOCK == 0 and nblk <= 32
    HP = N_HEADS // 2
    tq = min(2048, S)
    slopes = jnp.exp2(-8.0 * jnp.arange(1, N_HEADS + 1, dtype=F32) / N_HEADS)
    sl = jnp.zeros((HP, SUBLANES, LANES), F32)
    sl = sl.at[:, 0, :].set(slopes[0::2, None]).at[:, 1, :].set(slopes[1::2, None])
    gq = jnp.tile(q_norm_g.astype(F32), 2)[None, :]
    gk = jnp.tile(k_norm_g.astype(F32), 2)[None, :]
    qb, kb, vb = COL_Q // LANES, COL_K // LANES, COL_V // LANES
    grid = (B, HP, S // tq)
    sem3 = ("parallel", "parallel", "parallel")

    nbt = tq // MOBA_BLOCK
    tk = MOBA_KEY_GROUP * MOBA_BLOCK
    assert tq % tk == 0
    kaug, kmean, vt = pl.pallas_call(
        functools.partial(_kprep_kernel, tq=tq, tk=tk),
        grid=grid,
        in_specs=[
            pl.BlockSpec((None, tq, LANES), lambda b, p, s: (b, s, kb + p)),
            pl.BlockSpec((None, tq, LANES), lambda b, p, s: (b, s, vb + p)),
            pl.BlockSpec((1, LANES), lambda b, p, s: (0, 0)),
            pl.BlockSpec((None, SUBLANES, LANES), lambda b, p, s: (p, 0, 0)),
        ],
        out_specs=[
            pl.BlockSpec((None, 2, tq, LANES), lambda b, p, s: (b, p, s, 0)),
            pl.BlockSpec((None, 2, nbt, LANES), lambda b, p, s: (b, p, s, 0)),
            pl.BlockSpec((None, None, tq // tk, LANES, tk), lambda b, p, s: (b, p, s, 0, 0)),
        ],
        out_shape=[
            jax.ShapeDtypeStruct((B, N_HEADS, S, LANES), BF16),
            jax.ShapeDtypeStruct((B, N_HEADS, nblk, LANES), F32),
            jax.ShapeDtypeStruct((B, HP, S // tk, LANES, tk), BF16),
        ],
        compiler_params=_cparams(sem3),
        name="moba_kprep",
    )(proj3, proj3, gk, sl)

    qaug = pl.pallas_call(
        functools.partial(_qprep_kernel, tq=tq, nblk=nblk),
        grid=grid,
        in_specs=[
            pl.BlockSpec((None, tq, LANES), lambda b, p, s: (b, s, qb + p)),
            pl.BlockSpec((1, LANES), lambda b, p, s: (0, 0)),
            pl.BlockSpec((None, SUBLANES, LANES), lambda b, p, s: (p, 0, 0)),
            pl.BlockSpec((None, 2, nblk, LANES), lambda b, p, s: (b, p, 0, 0)),
        ],
        out_specs=pl.BlockSpec((None, 2, tq, LANES), lambda b, p, s: (b, p, s, 0)),
        out_shape=jax.ShapeDtypeStruct((B, N_HEADS, S, LANES), BF16),
        compiler_params=_cparams(sem3),
        name="moba_qprep",
    )(proj3, gq, sl, kmean)

    ta = min(MOBA_Q_TILE, S)
    assert tk % ta == 0
    return pl.pallas_call(
        functools.partial(_attn_kernel, tq=ta, tk=tk),
        grid=(B, HP, S // ta),
        in_specs=[
            pl.BlockSpec((None, 2, ta, LANES), lambda b, p, i: (b, p, i, 0)),
            pl.BlockSpec((None, 2, S, LANES), lambda b, p, i: (b, p, 0, 0)),
            pl.BlockSpec((None, None, S // tk, LANES, tk), lambda b, p, i: (b, p, 0, 0, 0)),
        ],
        out_specs=pl.BlockSpec((None, ta, LANES), lambda b, p, i: (b, i, p)),
        out_shape=jax.ShapeDtypeStruct((B, S, ATTN_WIDTH), BF16),
        scratch_shapes=[pltpu.VMEM((2, LANES, ta), F32), pltpu.VMEM((2, 1, ta), F32)],
        compiler_params=_cparams(("parallel", "parallel", "arbitrary")),
        name="moba_attn",
    )(qaug, kaug, vt)


def _silu(x):
    return x * (1.0 / (1.0 + jnp.exp(-x)))


def _conv_silu(u_ref, halo_ref, w_ref, b_ref, first):
    u = u_ref[...].astype(F32)
    halo = jnp.where(first, 0.0, halo_ref[...].astype(F32))
    ext = jnp.concatenate([halo, u], axis=0)
    w = w_ref[...]
    out = b_ref[...] + w[3:4, :] * u
    for back in range(1, SSD_CONV):
        out = out + w[3 - back:4 - back, :] * pltpu.roll(ext, back, axis=0)[SUBLANES:, :]
    return _silu(out)


def _ssd_group(xs, bm, cm, z, dt, rg, a, d, ng, state_ref):
    L, W = xs.shape
    hp = lax.Precision.HIGHEST
    dtx = jnp.dot(dt, rg, precision=hp, preferred_element_type=F32)
    ax = dtx * a
    rr = lax.broadcasted_iota(I32, (L, L), 0)
    cc = lax.broadcasted_iota(I32, (L, L), 1)
    causal = cc <= rr
    acum = jnp.dot(causal.astype(F32), ax, precision=hp, preferred_element_type=F32)
    acum_t = acum.T
    a_last = acum[L - 1:L, :]
    xdt = xs * dtx
    cmb = cm.astype(BF16)
    cb = lax.dot_general(cmb, bm.astype(BF16), NT_DIMS, preferred_element_type=F32)
    lane = lax.broadcasted_iota(I32, (L, W), 1)
    y = jnp.zeros((L, W), F32)
    for r in range(W // 64):
        seg = acum[:, 64 * r:64 * r + 1] - acum_t[64 * r:64 * r + 1, :]
        lmat = jnp.exp(jnp.where(causal, seg, -jnp.inf))
        xr = jnp.where((lane >= 64 * r) & (lane < 64 * r + 64), xdt, 0.0).astype(BF16)
        y = y + jnp.dot((cb * lmat).astype(BF16), xr, preferred_element_type=F32)
    state = state_ref[...]
    y = y + jnp.dot(cmb, state.astype(BF16), preferred_element_type=F32) * jnp.exp(acum)
    wgt = (xdt * jnp.exp(a_last - acum)).astype(BF16)
    state_ref[...] = state * jnp.exp(a_last) + jnp.dot(bm.T.astype(BF16), wgt, preferred_element_type=F32)
    y = y + d * xs
    y = y * _silu(z)
    y = y * lax.rsqrt(jnp.mean(y * y, axis=-1, keepdims=True) + EPS)
    return y * ng


def _ssd_kernel(x_ref, xh_ref, b_ref, bh_ref, c_ref, ch_ref, z_ref, dt_ref,
                wx_ref, wb_ref, wc_ref, bx_ref, bb_ref, bc_ref, dtb_ref, rg_ref,
                a_ref, d_ref, ng_ref, y_ref, state_ref):
    first = pl.program_id(2) == 0
    W, N = SSD_GROUP_W, SSD_STATE

    @pl.when(first)
    def _():
        state_ref[...] = jnp.zeros_like(state_ref)

    xs = _conv_silu(x_ref, xh_ref, wx_ref, bx_ref, first)
    bm = _conv_silu(b_ref, bh_ref, wb_ref, bb_ref, first)
    cm = _conv_silu(c_ref, ch_ref, wc_ref, bc_ref, first)
    dt = jax.nn.softplus(dt_ref[...] + dtb_ref[...])
    z = z_ref[...].astype(F32)
    for gi in range(SSD_GROUPS_PER_STEP):
        cw, cn = slice(gi * W, (gi + 1) * W), slice(gi * N, (gi + 1) * N)
        y = _ssd_group(xs[:, cw], bm[:, cn], cm[:, cn], z[:, cw], dt, rg_ref[gi],
                       a_ref[:, cw], d_ref[:, cw], ng_ref[:, cw], state_ref.at[gi])
        y_ref[:, cw] = y.astype(y_ref.dtype)


def ssd_mixer(proj3, dtraw3, conv_w, conv_b, dt_bias, a_log, d_skip, norm_g):
    B, S, _ = proj3.shape
    GP = SSD_GROUPS_PER_STEP
    L, W, N = SSD_CHUNK, SSD_GROUP_W * GP, SSD_STATE * GP
    assert S % L == 0 and SSD_GROUPS % GP == 0
    nc = S // L
    hb = L // SUBLANES
    xb, bb, cb_, zb = COL_X // W, COL_B // N, COL_C // N, COL_Z // W
    rep = SSD_INNER // SSD_HEADS
    a_exp = jnp.repeat(-jnp.exp(a_log.astype(F32)), rep)[None, :]
    d_exp = jnp.repeat(d_skip.astype(F32), rep)[None, :]
    ng = norm_g.astype(F32)[None, :]
    dtb = jnp.zeros((1, LANES), F32).at[0, :SSD_HEADS].set(dt_bias.astype(F32))
    head_of_chan = np.arange(SSD_INNER) // rep
    rg = (np.arange(LANES)[None, :, None]
          == head_of_chan.reshape(SSD_GROUPS, 1, SSD_GROUP_W)).astype(np.float32)
    cw = conv_w.astype(F32)
    cbias = conv_b.astype(F32)[None, :]
    cxo, cbo, cco = 0, SSD_INNER // N, (SSD_INNER + SSD_GN) // N

    def halo(col):
        return lambda b, g, c: (b, jnp.maximum(c * hb - 1, 0), col + g)

    return pl.pallas_call(
        _ssd_kernel,
        grid=(B, SSD_GROUPS // GP, nc),
        in_specs=[
            pl.BlockSpec((None, L, W), lambda b, g, c: (b, c, xb + g)),
            pl.BlockSpec((None, SUBLANES, W), halo(xb)),
            pl.BlockSpec((None, L, N), lambda b, g, c: (b, c, bb + g)),
            pl.BlockSpec((None, SUBLANES, N), halo(bb)),
            pl.BlockSpec((None, L, N), lambda b, g, c: (b, c, cb_ + g)),
            pl.BlockSpec((None, SUBLANES, N), halo(cb_)),
            pl.BlockSpec((None, L, W), lambda b, g, c: (b, c, zb + g)),
            pl.BlockSpec((None, L, LANES), lambda b, g, c: (b, c, 0)),
            pl.BlockSpec((SSD_CONV, W), lambda b, g, c: (0, cxo + g)),
            pl.BlockSpec((SSD_CONV, N), lambda b, g, c: (0, cbo + g)),
            pl.BlockSpec((SSD_CONV, N), lambda b, g, c: (0, cco + g)),
            pl.BlockSpec((1, W), lambda b, g, c: (0, cxo + g)),
            pl.BlockSpec((1, N), lambda b, g, c: (0, cbo + g)),
            pl.BlockSpec((1, N), lambda b, g, c: (0, cco + g)),
            pl.BlockSpec((1, LANES), lambda b, g, c: (0, 0)),
            pl.BlockSpec((GP, LANES, SSD_GROUP_W), lambda b, g, c: (g, 0, 0)),
            pl.BlockSpec((1, W), lambda b, g, c: (0, g)),
            pl.BlockSpec((1, W), lambda b, g, c: (0, g)),
            pl.BlockSpec((1, W), lambda b, g, c: (0, g)),
        ],
        out_specs=pl.BlockSpec((None, L, W), lambda b, g, c: (b, c, g)),
        out_shape=jax.ShapeDtypeStruct((B, S, SSD_INNER), BF16),
        scratch_shapes=[pltpu.VMEM((GP, SSD_STATE, SSD_GROUP_W), F32)],
        compiler_params=_cparams(("parallel", "parallel", "arbitrary")),
        name="ssd_scan",
    )(proj3, proj3, proj3, proj3, proj3, proj3, proj3, dtraw3,
      cw, cw, cw, cbias, cbias, cbias, dtb, jnp.asarray(rg), a_exp, d_exp, ng)


def _merge_kernel(x_ref, a_ref, s_ref, ga_ref, gs_ref, wa_ref, ws_ref, wo_ref, o_ref):
    ya = jnp.dot(a_ref[...], wa_ref[...], preferred_element_type=F32)
    ys = jnp.dot(s_ref[...], ws_ref[...], preferred_element_type=F32)
    mixed = jax.nn.sigmoid(ga_ref[...].astype(F32)) * ya + jax.nn.sigmoid(gs_ref[...].astype(F32)) * ys
    o_ref[...] = x_ref[...] + jnp.dot(mixed.astype(BF16), wo_ref[...], preferred_element_type=F32)


def merge_branches(x2, attn2, ssd2, proj2, wa, ws, wo, tm):
    T, D = x2.shape
    full = lambda a: pl.BlockSpec(a.shape, lambda i: (0, 0))
    return pl.pallas_call(
        _merge_kernel,
        grid=(T // tm,),
        in_specs=[
            pl.BlockSpec((tm, D), lambda i: (i, 0)),
            pl.BlockSpec((tm, ATTN_WIDTH), lambda i: (i, 0)),
            pl.BlockSpec((tm, SSD_INNER), lambda i: (i, 0)),
            pl.BlockSpec((tm, D), lambda i: (i, COL_GA // D)),
            pl.BlockSpec((tm, D), lambda i: (i, COL_GS // D)),
            full(wa), full(ws), full(wo),
        ],
        out_specs=pl.BlockSpec((tm, D), lambda i: (i, 0)),
        out_shape=jax.ShapeDtypeStruct((T, D), F32),
        compiler_params=_cparams(("parallel",)),
        name="merge",
    )(x2, attn2, ssd2, proj2, proj2, wa, ws, wo)


def _topk_rows(s, idx, k):
    n = s.shape[0]
    row = lax.broadcasted_iota(I32, s.shape, 0)
    vals, rows, picked = [], [], []
    for _ in range(k):
        m = jnp.max(s, axis=0, keepdims=True)
        first = jnp.min(jnp.where(s == m, row, n), axis=0, keepdims=True)
        hit = row == first
        vals.append(m)
        rows.append(first)
        if idx is not None:
            picked.append(jnp.max(jnp.where(hit, idx, -1), axis=0, keepdims=True))
        s = jnp.where(hit, -jnp.inf, s)
    cat = lambda xs: jnp.concatenate(xs, axis=0)
    return cat(vals), cat(rows), (cat(picked) if idx is not None else None)


def _peer_topk_kernel(q_ref, k1_ref, k2_ref, eidx_ref, gw_ref, *, tt):
    half = PEER_QDIM // 2
    e_rows, g_rows = [], []
    for h in range(PEER_HEADS):
        qa = q_ref[:, h * PEER_QDIM:h * PEER_QDIM + half].astype(BF16)
        qb = q_ref[:, h * PEER_QDIM + half:(h + 1) * PEER_QDIM].astype(BF16)
        s1 = lax.dot_general(k1_ref[h], qa, NT_DIMS, preferred_element_type=F32)
        s2 = lax.dot_general(k2_ref[h], qb, NT_DIMS, preferred_element_type=F32)
        v1, i1, _ = _topk_rows(s1, None, PEER_TOPK)
        v2, i2, _ = _topk_rows(s2, None, PEER_TOPK)
        sub = lax.broadcasted_iota(I32, (SUBLANES, tt), 0)
        cand_parts = [v1[0:1, :] + v2]
        cidx_parts = [i1[0:1, :] * PEER_NKEYS + i2]
        for a in range(1, SUBLANES):
            keep = sub < PEER_TOPK // (a + 1)
            cand_parts.append(jnp.where(keep, v1[a:a + 1, :] + v2[0:SUBLANES, :], -jnp.inf))
            cidx_parts.append(i1[a:a + 1, :] * PEER_NKEYS + i2[0:SUBLANES, :])
        cand_parts.append(v1[SUBLANES:, :] + v2[0:1, :])
        cidx_parts.append(i1[SUBLANES:, :] * PEER_NKEYS + i2[0:1, :])
        cand = jnp.concatenate(cand_parts, axis=0)
        cidx = jnp.concatenate(cidx_parts, axis=0)
        sv, _, ex = _topk_rows(cand, cidx, PEER_TOPK)
        e = jnp.exp(sv - sv[0:1, :])
        g_rows.append(e / jnp.sum(e, axis=0, keepdims=True))
        e_rows.append(ex)
    eidx_ref[...] = jnp.concatenate(e_rows, axis=0).T
    gw = jnp.concatenate(g_rows, axis=0)
    for part in range(tt // LANES):
        gw_ref[part] = gw[:, part * LANES:(part + 1) * LANES]


def peer_topk(q, keys1, keys2, tt):
    T = q.shape[0]
    full3 = lambda a: pl.BlockSpec(a.shape, lambda i: (0, 0, 0))
    return pl.pallas_call(
        functools.partial(_peer_topk_kernel, tt=tt),
        grid=(T // tt,),
        in_specs=[pl.BlockSpec((tt, q.shape[1]), lambda i: (i, 0)), full3(keys1), full3(keys2)],
        out_specs=[
            pl.BlockSpec((tt, PEER_SLOTS), lambda i: (i, 0)),
            pl.BlockSpec((tt // LANES, PEER_SLOTS, LANES), lambda i: (i, 0, 0)),
        ],
        out_shape=[
            jax.ShapeDtypeStruct((T, PEER_SLOTS), I32),
            jax.ShapeDtypeStruct((T // LANES, PEER_SLOTS, LANES), F32),
        ],
        compiler_params=_cparams(("parallel",)),
        name="peer_topk",
    )(q, keys1, keys2)


def pack_table(tab):
    half = tab.shape[1] // 2
    bits = lax.bitcast_convert_type(tab.astype(BF16), jnp.uint16).astype(jnp.uint32)
    return lax.bitcast_convert_type((bits[:, :half] << 16) | bits[:, half:], I32)


def sc_gather_rows(table, idx, win=64):
    V, W = table.shape
    N = idx.shape[0]
    info = plsc.get_sparse_core_info()
    n_cores, n_sub = info.num_cores, info.num_subcores
    workers = n_cores * n_sub
    per_w = N // workers
    steps = per_w // win
    assert steps * win * workers == N and steps % 2 == 0
    mesh = plsc.VectorSubcoreMesh(core_axis_name="c", subcore_axis_name="s")
    dma = pltpu.SemaphoreType.DMA

    @functools.partial(
        pl.kernel, mesh=mesh,
        out_type=jax.ShapeDtypeStruct((N, W), table.dtype),
        scratch_types=[
            pltpu.VMEM((steps, win), I32),
            pltpu.VMEM((win, W), table.dtype),
            pltpu.VMEM((win, W), table.dtype),
            dma, dma, dma, dma,
        ],
    )
    def gather_kernel(table_hbm, idx_hbm, out_hbm, idx_v, rows0, rows1, g0, g1, w0, w1):
        wid = lax.axis_index("s") * n_cores + lax.axis_index("c")
        row0 = wid * steps
        pltpu.sync_copy(idx_hbm.at[pl.ds(row0, steps)], idx_v)
        slots = ((rows0, g0, w0), (rows1, g1, w1))

        def gather(i, slot):
            rows, gsem, _ = slots[slot]
            return pltpu.make_async_copy(table_hbm.at[idx_v.at[i]], rows, gsem)

        def write(i, slot):
            rows, _, wsem = slots[slot]
            off = pl.multiple_of((row0 + i) * win, win)
            return pltpu.make_async_copy(rows, out_hbm.at[pl.ds(off, win)], wsem)

        gather(0, 0).start()

        @pl.loop(0, steps, step=2)
        def _(i):
            @pl.when(i > 0)
            def _():
                write(i - 1, 1).wait()

            gather(i + 1, 1).start()
            gather(i, 0).wait()
            write(i, 0).start()
            write(i, 0).wait()

            @pl.when(i + 2 < steps)
            def _():
                gather(i + 2, 0).start()

            gather(i + 1, 1).wait()
            write(i + 1, 1).start()

        write(steps - 1, 1).wait()

    return gather_kernel(table, idx.reshape(N // win, win))


def _unpack_words(w):
    u = pltpu.bitcast(w, jnp.uint32)
    hi = pltpu.bitcast(u & jnp.uint32(0xFFFF0000), F32)
    lo = pltpu.bitcast(u << 16, F32)
    return hi, lo


def _peer_expert_kernel(x_ref, g_ref, ug_ref, vg_ref, gw_ref, o_ref, *, tt):
    i = pl.program_id(0)
    half = D_MODEL // 2
    x1 = x_ref[...]
    xn = x1 * lax.rsqrt(jnp.mean(x1 * x1, axis=-1, keepdims=True) + EPS) * g_ref[...]
    lane = lax.broadcasted_iota(I32, (PEER_SLOTS, LANES), 1)
    off = (i % (LANES // tt)) * tt
    act = jnp.zeros((PEER_SLOTS, LANES), F32)
    for t in range(tt):
        hi, lo = _unpack_words(ug_ref[t * PEER_SLOTS:(t + 1) * PEER_SLOTS, :])
        prod = hi * xn[t:t + 1, :half] + lo * xn[t:t + 1, half:]
        fold = prod[:, 0:LANES]
        for c in range(1, half // LANES):
            fold = fold + prod[:, c * LANES:(c + 1) * LANES]
        col = jnp.sum(fold, axis=-1, keepdims=True)
        act = jnp.where(lane == off + t, col, act)
    gelu = 0.5 * act * (1.0 + lax.erf(act * (2.0 ** -0.5)))
    hact = gelu * gw_ref[...]
    for t in range(tt):
        hcol = jnp.sum(jnp.where(lane == off + t, hact, 0.0), axis=-1, keepdims=True)
        hi, lo = _unpack_words(vg_ref[t * PEER_SLOTS:(t + 1) * PEER_SLOTS, :])
        o_ref[t:t + 1, :half] = x1[t:t + 1, :half] + jnp.sum(hcol * hi, axis=0, keepdims=True)
        o_ref[t:t + 1, half:] = x1[t:t + 1, half:] + jnp.sum(hcol * lo, axis=0, keepdims=True)


def peer_experts(x1, g2, ug, vg, gw, tt=32):
    T, D = x1.shape
    W = ug.shape[1]
    per = LANES // tt
    return pl.pallas_call(
        functools.partial(_peer_expert_kernel, tt=tt),
        grid=(T // tt,),
        in_specs=[
            pl.BlockSpec((tt, D), lambda i: (i, 0)),
            pl.BlockSpec((1, D), lambda i: (0, 0)),
            pl.BlockSpec((tt * PEER_SLOTS, W), lambda i: (i, 0)),
            pl.BlockSpec((tt * PEER_SLOTS, W), lambda i: (i, 0)),
            pl.BlockSpec((None, PEER_SLOTS, LANES), lambda i: (i // per, 0, 0)),
        ],
        out_specs=pl.BlockSpec((tt, D), lambda i: (i, 0)),
        out_shape=jax.ShapeDtypeStruct((T, D), F32),
        compiler_params=_cparams(("parallel",)),
        name="peer_experts",
    )(x1, g2, ug, vg, gw)


def peer_ffn_residual(x1, norm2_g, wq, keys1, keys2, u_packed, v_packed):
    T = x1.shape[0]
    g2 = norm2_g.astype(F32)[None, :]
    q = norm_matmul(x1, g2, wq, F32, min(1024, T), 1024)
    eidx, gw = peer_topk(q, keys1, keys2, min(256, T))
    flat = eidx.reshape(-1)
    ug = sc_gather_rows(u_packed, flat)
    vg = sc_gather_rows(v_packed, flat)
    return peer_experts(x1, g2, ug, vg, gw)


def kernel(x, norm1_g, w_in, q_norm_g, k_norm_g, conv_w, conv_b, dt_bias, a_log, d_skip, ssd_norm_g,
           w_attn_o, w_ssd_o, w_out, norm2_g, w_peer_q, peer_keys1, peer_keys2, peer_u, peer_v):
    B, S, D = x.shape
    xs = [x[b] for b in range(B)]
    for l in range(norm1_g.shape[0]):
        w = w_in[l]
        dt0 = COL_GA
        w_main = jnp.concatenate([w[:, :dt0], w[:, dt0 + SSD_HEADS:]], axis=1).astype(BF16)
        w_dt = jnp.zeros((D, LANES), BF16).at[:, :SSD_HEADS].set(w[:, dt0:dt0 + SSD_HEADS].astype(BF16))
        g1 = norm1_g[l].astype(F32)[None, :]
        wa, ws, wo = w_attn_o[l].astype(BF16), w_ssd_o[l].astype(BF16), w_out[l].astype(BF16)
        wq = w_peer_q[l].astype(BF16)
        k1, k2 = peer_keys1[l].astype(BF16), peer_keys2[l].astype(BF16)
        up, vp = pack_table(peer_u[l]), pack_table(peer_v[l])
        tm = min(1024, S)
        for b in range(B):
            x2 = xs[b]
            proj = norm_matmul(x2, g1, w_main, BF16, tm, 1024)
            dtraw = norm_matmul(x2, g1, w_dt, F32, tm, LANES)
            proj3 = proj[None]
            attn = moba_attention(proj3, q_norm_g[l], k_norm_g[l])
            yssd = ssd_mixer(proj3, dtraw[None], conv_w[l], conv_b[l], dt_bias[l],
                             a_log[l], d_skip[l], ssd_norm_g[l])
            x1 = merge_branches(x2, attn[0], yssd[0], proj, wa, ws, wo, min(512, S))
            xs[b] = peer_ffn_residual(x1, norm2_g[l], wq, k1, k2, up, vp)
    return jnp.stack(xs, axis=0)
```

```python
import functools

import jax
import jax.numpy as jnp
import numpy as np
from jax import lax
from jax.experimental import pallas as pl
from jax.experimental.pallas import tpu as pltpu
from jax.experimental.pallas import tpu_sc as plsc

F32 = jnp.float32
BF16 = jnp.bfloat16
I32 = jnp.int32

EPS = 1e-6
D_MODEL = 1024
N_HEADS = 16
HEAD_DIM = 64
ATTN_WIDTH = N_HEADS * HEAD_DIM
MOBA_BLOCK = 256
MOBA_TOPK = 3
MOBA_KEY_GROUP = 4
MOBA_Q_TILE = 1024
SSD_INNER = 2048
SSD_HEADS = 32
SSD_GROUPS = 8
SSD_STATE = 128
SSD_CONV = 4
SSD_CHUNK = 256
SSD_GN = SSD_GROUPS * SSD_STATE
SSD_GROUP_W = SSD_INNER // SSD_GROUPS
SSD_GROUPS_PER_STEP = 2
PEER_HEADS = 8
PEER_NKEYS = 128
PEER_QDIM = 256
PEER_TOPK = 16
PEER_SLOTS = PEER_HEADS * PEER_TOPK
PEER_CHUNKS = 4

LANES = 128
SUBLANES = 8
VMEM_LIMIT = 56 * 1024 * 1024
MASK_BIG = 1e30

COL_Q, COL_K, COL_V = 0, ATTN_WIDTH, 2 * ATTN_WIDTH
COL_Z = 3 * ATTN_WIDTH
COL_X = COL_Z + SSD_INNER
COL_B = COL_X + SSD_INNER
COL_C = COL_B + SSD_GN
COL_GA = COL_C + SSD_GN
COL_GS = COL_GA + D_MODEL
PROJ_COLS = COL_GS + D_MODEL

NT_DIMS = (((1,), (1,)), ((), ()))


def _cparams(sem):
    return pltpu.CompilerParams(dimension_semantics=sem, vmem_limit_bytes=VMEM_LIMIT)


def _norm_matmul_kernel(x_ref, g_ref, w_ref, o_ref, h_ref):
    @pl.when(pl.program_id(1) == 0)
    def _():
        x = x_ref[...]
        y = x * lax.rsqrt(jnp.mean(x * x, axis=-1, keepdims=True) + EPS)
        h_ref[...] = (y * g_ref[...]).astype(h_ref.dtype)

    o_ref[...] = jnp.dot(h_ref[...], w_ref[...], preferred_element_type=F32).astype(o_ref.dtype)


def norm_matmul(x, g, w, out_dtype, tm, tn):
    T, K = x.shape
    N = w.shape[1]
    return pl.pallas_call(
        _norm_matmul_kernel,
        grid=(T // tm, N // tn),
        in_specs=[
            pl.BlockSpec((tm, K), lambda i, j: (i, 0)),
            pl.BlockSpec((1, K), lambda i, j: (0, 0)),
            pl.BlockSpec((K, tn), lambda i, j: (0, j)),
        ],
        out_specs=pl.BlockSpec((tm, tn), lambda i, j: (i, j)),
        out_shape=jax.ShapeDtypeStruct((T, N), out_dtype),
        scratch_shapes=[pltpu.VMEM((tm, K), BF16)],
        compiler_params=_cparams(("parallel", "arbitrary")),
        name="norm_matmul",
    )(x, g, w)


def _split3(v):
    hi = v.astype(BF16).astype(F32)
    r1 = v - hi
    mid = r1.astype(BF16).astype(F32)
    return hi, mid, r1 - mid


def _head_pair_norm(x, g):
    lane = lax.broadcasted_iota(I32, x.shape, 1)
    low = lane < HEAD_DIM
    x2 = x * x
    ss_a = jnp.sum(jnp.where(low, x2, 0.0), axis=-1, keepdims=True)
    ss_b = jnp.sum(jnp.where(low, 0.0, x2), axis=-1, keepdims=True)
    inv = jnp.where(low, lax.rsqrt(ss_a / HEAD_DIM + EPS), lax.rsqrt(ss_b / HEAD_DIM + EPS))
    return x * inv * g


def _kprep_kernel(k_ref, v_ref, g_ref, sl_ref, kaug_ref, kmean_ref, vt_ref, *, tq, tk):
    s_idx = pl.program_id(2)
    for grp in range(tq // tk):
        vblk = v_ref[grp * tk:(grp + 1) * tk, :].astype(F32)
        vt_ref[grp] = vblk.T.astype(BF16)
    kn = _head_pair_norm(k_ref[...].astype(F32), g_ref[...])
    nb = tq // MOBA_BLOCK
    km = jnp.mean(kn.reshape(nb, MOBA_BLOCK, LANES), axis=1)
    lane = lax.broadcasted_iota(I32, (tq, LANES), 1)
    row = lax.broadcasted_iota(I32, (tq, LANES), 0) + s_idx * tq
    blk = row // MOBA_BLOCK
    pos = row.astype(F32)
    lane_m = lax.broadcasted_iota(I32, (nb, LANES), 1)
    heads = ((kn, km), (pltpu.roll(kn, HEAD_DIM, axis=1), pltpu.roll(km, HEAD_DIM, axis=1)))
    for hh, (kk, kmm) in enumerate(heads):
        hi, mid, lo = _split3(sl_ref[hh:hh + 1, :] * pos)
        aug = jnp.where(lane < HEAD_DIM, kk, 0.0)
        aug = jnp.where((lane >= 64) & (lane < 96), (lane - 64 == blk).astype(F32), aug)
        aug = jnp.where(lane == 96, hi, aug)
        aug = jnp.where(lane == 97, mid, aug)
        aug = jnp.where(lane == 98, lo, aug)
        aug = jnp.where((lane >= 99) & (lane < 102), 1.0, aug)
        kaug_ref[hh] = aug.astype(BF16)
        kmean_ref[hh] = jnp.where(lane_m < HEAD_DIM, kmm, 0.0)


def _qprep_kernel(q_ref, g_ref, sl_ref, kmean_ref, qaug_ref, *, tq, nblk):
    s_idx = pl.program_id(2)
    qn = _head_pair_norm(q_ref[...].astype(F32), g_ref[...])
    lane = lax.broadcasted_iota(I32, (tq, LANES), 1)
    t = (lax.broadcasted_iota(I32, (tq, LANES), 0) + s_idx * tq).astype(F32)
    jrow = lax.broadcasted_iota(I32, (32, tq), 0)
    own = (lax.broadcasted_iota(I32, (32, tq), 1) + s_idx * tq) // MOBA_BLOCK
    heads = (qn, pltpu.roll(qn, HEAD_DIM, axis=1))
    for hh, qh in enumerate(heads):
        qq = jnp.where(lane < HEAD_DIM, qh, 0.0)
        km_rows = jnp.concatenate([kmean_ref[hh], jnp.zeros((32 - nblk, LANES), F32)], axis=0) \
            if nblk < 32 else kmean_ref[hh]
        g = lax.dot_general(km_rows, qq, NT_DIMS, precision=lax.Precision.HIGHEST,
                            preferred_element_type=F32)
        g = jnp.where(jrow < own, g, -jnp.inf)
        allowed = jrow == own
        for r in range(MOBA_TOPK):
            m = jnp.max(g, axis=0, keepdims=True)
            first = jnp.min(jnp.where(g == m, jrow, 1 << 20), axis=0, keepdims=True)
            hit = jrow == first
            allowed = allowed | (hit & (own > r))
            g = jnp.where(hit, -jnp.inf, g)
        mask_t = jnp.where(allowed, 0.0, -MASK_BIG)
        mask = jnp.concatenate([jnp.zeros((64, tq), F32), mask_t, jnp.zeros((32, tq), F32)], axis=0).T
        hi, mid, lo = _split3(-sl_ref[hh:hh + 1, :] * t)
        aug = qq * (HEAD_DIM ** -0.5)
        aug = jnp.where((lane >= 64) & (lane < 96), mask, aug)
        aug = jnp.where((lane >= 96) & (lane < 99), 1.0, aug)
        aug = jnp.where(lane == 99, hi, aug)
        aug = jnp.where(lane == 100, mid, aug)
        aug = jnp.where(lane == 101, lo, aug)
        qaug_ref[hh] = aug.astype(BF16)


def _attn_kernel(q_ref, k_ref, vt_ref, o_ref, acc_ref, m_ref, *, tq, tk, q_off):
    i = pl.program_id(2) + q_off
    vrow = lax.broadcasted_iota(I32, (LANES, tk), 0)
    low = vrow < HEAD_DIM
    last = (i * tq) // tk

    def group(g, diag):
        start = pl.multiple_of(g * tk, tk)
        vt = vt_ref[g]
        one = jnp.ones_like(vt)
        if diag:
            kpos = lax.broadcasted_iota(I32, (tk, tq), 0) + g * tk
            qpos = lax.broadcasted_iota(I32, (tk, tq), 1) + i * tq
            causal = kpos <= qpos
        for hh in range(2):
            kj = k_ref[hh, pl.ds(start, tk), :]
            s = lax.dot_general(kj, q_ref[hh], NT_DIMS, preferred_element_type=F32)
            if diag:
                s = jnp.where(causal, s, -MASK_BIG)
            m_old = m_ref[hh]
            m_new = jnp.maximum(m_old, jnp.max(s, axis=0, keepdims=True))
            alpha = jnp.exp(m_old - m_new)
            p = jnp.exp(s - m_new).astype(BF16)
            vaug = jnp.where(low, vt, one) if hh == 0 else jnp.where(low, one, vt)
            acc_ref[hh] = alpha * acc_ref[hh] + jnp.dot(vaug, p, preferred_element_type=F32)
            m_ref[hh] = m_new

    acc_ref[...] = jnp.zeros_like(acc_ref)
    m_ref[...] = jnp.full_like(m_ref, -MASK_BIG)
    group(last, True)

    def body(g, carry):
        group(g, False)
        return carry

    lax.fori_loop(0, last, body, 0)
    a = acc_ref[0]
    b = acc_ref[1]
    low_q = lax.broadcasted_iota(I32, (LANES, tq), 0) < HEAD_DIM
    out = jnp.where(low_q, a / a[HEAD_DIM:HEAD_DIM + 1, :], b / b[0:1, :])
    o_ref[...] = out.T.astype(o_ref.dtype)


def moba_prepare(proj3, q_norm_g, k_norm_g):
    B, S, _ = proj3.shape
    nblk = S // MOBA_BLOCK
    assert S % MOBA_BLOCK == 0 and nblk <= 32
    HP = N_HEADS // 2
    tq = min(2048, S)
    slopes = jnp.exp2(-8.0 * jnp.arange(1, N_HEADS + 1, dtype=F32) / N_HEADS)
    sl = jnp.zeros((HP, SUBLANES, LANES), F32)
    sl = sl.at[:, 0, :].set(slopes[0::2, None]).at[:, 1, :].set(slopes[1::2, None])
    gq = jnp.tile(q_norm_g.astype(F32), 2)[None, :]
    gk = jnp.tile(k_norm_g.astype(F32), 2)[None, :]
    qb, kb, vb = COL_Q // LANES, COL_K // LANES, COL_V // LANES
    grid = (B, HP, S // tq)
    sem3 = ("parallel", "parallel", "parallel")

    nbt = tq // MOBA_BLOCK
    tk = MOBA_KEY_GROUP * MOBA_BLOCK
    assert tq % tk == 0
    kaug, kmean, vt = pl.pallas_call(
        functools.partial(_kprep_kernel, tq=tq, tk=tk),
        grid=grid,
        in_specs=[
            pl.BlockSpec((None, tq, LANES), lambda b, p, s: (b, s, kb + p)),
            pl.BlockSpec((None, tq, LANES), lambda b, p, s: (b, s, vb + p)),
            pl.BlockSpec((1, LANES), lambda b, p, s: (0, 0)),
            pl.BlockSpec((None, SUBLANES, LANES), lambda b, p, s: (p, 0, 0)),
        ],
        out_specs=[
            pl.BlockSpec((None, 2, tq, LANES), lambda b, p, s: (b, p, s, 0)),
            pl.BlockSpec((None, 2, nbt, LANES), lambda b, p, s: (b, p, s, 0)),
            pl.BlockSpec((None, None, tq // tk, LANES, tk), lambda b, p, s: (b, p, s, 0, 0)),
        ],
        out_shape=[
            jax.ShapeDtypeStruct((B, N_HEADS, S, LANES), BF16),
            jax.ShapeDtypeStruct((B, N_HEADS, nblk, LANES), F32),
            jax.ShapeDtypeStruct((B, HP, S // tk, LANES, tk), BF16),
        ],
        compiler_params=_cparams(sem3),
        name="moba_kprep",
    )(proj3, proj3, gk, sl)

    qaug = pl.pallas_call(
        functools.partial(_qprep_kernel, tq=tq, nblk=nblk),
        grid=grid,
        in_specs=[
            pl.BlockSpec((None, tq, LANES), lambda b, p, s: (b, s, qb + p)),
            pl.BlockSpec((1, LANES), lambda b, p, s: (0, 0)),
            pl.BlockSpec((None, SUBLANES, LANES), lambda b, p, s: (p, 0, 0)),
            pl.BlockSpec((None, 2, nblk, LANES), lambda b, p, s: (b, p, 0, 0)),
        ],
        out_specs=pl.BlockSpec((None, 2, tq, LANES), lambda b, p, s: (b, p, s, 0)),
        out_shape=jax.ShapeDtypeStruct((B, N_HEADS, S, LANES), BF16),
        compiler_params=_cparams(sem3),
        name="moba_qprep",
    )(proj3, gq, sl, kmean)

    return qaug, kaug, vt


def moba_attention(qaug, kaug, vt, row_lo, row_hi):
    B, _, S, _ = qaug.shape
    HP = N_HEADS // 2
    tk = MOBA_KEY_GROUP * MOBA_BLOCK
    ta = min(MOBA_Q_TILE, row_hi - row_lo)
    assert tk % ta == 0 and row_lo % ta == 0 and (row_hi - row_lo) % ta == 0
    q_off = row_lo // ta
    return pl.pallas_call(
        functools.partial(_attn_kernel, tq=ta, tk=tk, q_off=q_off),
        grid=(B, HP, (row_hi - row_lo) // ta),
        in_specs=[
            pl.BlockSpec((None, 2, ta, LANES), lambda b, p, i: (b, p, i + q_off, 0)),
            pl.BlockSpec((None, 2, S, LANES), lambda b, p, i: (b, p, 0, 0)),
            pl.BlockSpec((None, None, S // tk, LANES, tk), lambda b, p, i: (b, p, 0, 0, 0)),
        ],
        out_specs=pl.BlockSpec((None, ta, LANES), lambda b, p, i: (b, i, p)),
        out_shape=jax.ShapeDtypeStruct((B, row_hi - row_lo, ATTN_WIDTH), BF16),
        scratch_shapes=[pltpu.VMEM((2, LANES, ta), F32), pltpu.VMEM((2, 1, ta), F32)],
        compiler_params=_cparams(("parallel", "parallel", "arbitrary")),
        name="moba_attn",
    )(qaug, kaug, vt)


def _silu(x):
    return x * (1.0 / (1.0 + jnp.exp(-x)))


def _conv_silu(u_ref, halo_ref, w_ref, b_ref, first):
    u = u_ref[...].astype(F32)
    halo = jnp.where(first, 0.0, halo_ref[...].astype(F32))
    ext = jnp.concatenate([halo, u], axis=0)
    w = w_ref[...]
    out = b_ref[...] + w[3:4, :] * u
    for back in range(1, SSD_CONV):
        out = out + w[3 - back:4 - back, :] * pltpu.roll(ext, back, axis=0)[SUBLANES:, :]
    return _silu(out)


def _ssd_group(xs, bm, cm, z, dt, rg, a, d, ng, state_ref):
    L, W = xs.shape
    hp = lax.Precision.HIGHEST
    dtx = jnp.dot(dt, rg, precision=hp, preferred_element_type=F32)
    ax = dtx * a
    rr = lax.broadcasted_iota(I32, (L, L), 0)
    cc = lax.broadcasted_iota(I32, (L, L), 1)
    causal = cc <= rr
    acum = jnp.dot(causal.astype(F32), ax, precision=hp, preferred_element_type=F32)
    acum_t = acum.T
    a_last = acum[L - 1:L, :]
    xdt = xs * dtx
    cmb = cm.astype(BF16)
    cb = lax.dot_general(cmb, bm.astype(BF16), NT_DIMS, preferred_element_type=F32)
    lane = lax.broadcasted_iota(I32, (L, W), 1)
    y = jnp.zeros((L, W), F32)
    for r in range(W // 64):
        seg = acum[:, 64 * r:64 * r + 1] - acum_t[64 * r:64 * r + 1, :]
        lmat = jnp.exp(jnp.where(causal, seg, -jnp.inf))
        xr = jnp.where((lane >= 64 * r) & (lane < 64 * r + 64), xdt, 0.0).astype(BF16)
        y = y + jnp.dot((cb * lmat).astype(BF16), xr, preferred_element_type=F32)
    state = state_ref[...]
    y = y + jnp.dot(cmb, state.astype(BF16), preferred_element_type=F32) * jnp.exp(acum)
    wgt = (xdt * jnp.exp(a_last - acum)).astype(BF16)
    state_ref[...] = state * jnp.exp(a_last) + jnp.dot(bm.T.astype(BF16), wgt, preferred_element_type=F32)
    y = y + d * xs
    y = y * _silu(z)
    y = y * lax.rsqrt(jnp.mean(y * y, axis=-1, keepdims=True) + EPS)
    return y * ng


def _ssd_kernel(x_ref, xh_ref, b_ref, bh_ref, c_ref, ch_ref, z_ref, dt_ref,
                wx_ref, wb_ref, wc_ref, bx_ref, bb_ref, bc_ref, dtb_ref, rg_ref,
                a_ref, d_ref, ng_ref, y_ref, state_ref):
    first = pl.program_id(2) == 0
    W, N = SSD_GROUP_W, SSD_STATE

    @pl.when(first)
    def _():
        state_ref[...] = jnp.zeros_like(state_ref)

    xs = _conv_silu(x_ref, xh_ref, wx_ref, bx_ref, first)
    bm = _conv_silu(b_ref, bh_ref, wb_ref, bb_ref, first)
    cm = _conv_silu(c_ref, ch_ref, wc_ref, bc_ref, first)
    dt = jax.nn.softplus(dt_ref[...] + dtb_ref[...])
    z = z_ref[...].astype(F32)
    for gi in range(SSD_GROUPS_PER_STEP):
        cw, cn = slice(gi * W, (gi + 1) * W), slice(gi * N, (gi + 1) * N)
        y = _ssd_group(xs[:, cw], bm[:, cn], cm[:, cn], z[:, cw], dt, rg_ref[gi],
                       a_ref[:, cw], d_ref[:, cw], ng_ref[:, cw], state_ref.at[gi])
        y_ref[:, cw] = y.astype(y_ref.dtype)


def ssd_mixer(proj3, dtraw3, conv_w, conv_b, dt_bias, a_log, d_skip, norm_g):
    B, S, _ = proj3.shape
    GP = SSD_GROUPS_PER_STEP
    L, W, N = SSD_CHUNK, SSD_GROUP_W * GP, SSD_STATE * GP
    assert S % L == 0 and SSD_GROUPS % GP == 0
    nc = S // L
    hb = L // SUBLANES
    xb, bb, cb_, zb = COL_X // W, COL_B // N, COL_C // N, COL_Z // W
    rep = SSD_INNER // SSD_HEADS
    a_exp = jnp.repeat(-jnp.exp(a_log.astype(F32)), rep)[None, :]
    d_exp = jnp.repeat(d_skip.astype(F32), rep)[None, :]
    ng = norm_g.astype(F32)[None, :]
    dtb = jnp.zeros((1, LANES), F32).at[0, :SSD_HEADS].set(dt_bias.astype(F32))
    head_of_chan = np.arange(SSD_INNER) // rep
    rg = (np.arange(LANES)[None, :, None]
          == head_of_chan.reshape(SSD_GROUPS, 1, SSD_GROUP_W)).astype(np.float32)
    cw = conv_w.astype(F32)
    cbias = conv_b.astype(F32)[None, :]
    cxo, cbo, cco = 0, SSD_INNER // N, (SSD_INNER + SSD_GN) // N

    def halo(col):
        return lambda b, g, c: (b, jnp.maximum(c * hb - 1, 0), col + g)

    return pl.pallas_call(
        _ssd_kernel,
        grid=(B, SSD_GROUPS // GP, nc),
        in_specs=[
            pl.BlockSpec((None, L, W), lambda b, g, c: (b, c, xb + g)),
            pl.BlockSpec((None, SUBLANES, W), halo(xb)),
            pl.BlockSpec((None, L, N), lambda b, g, c: (b, c, bb + g)),
            pl.BlockSpec((None, SUBLANES, N), halo(bb)),
            pl.BlockSpec((None, L, N), lambda b, g, c: (b, c, cb_ + g)),
            pl.BlockSpec((None, SUBLANES, N), halo(cb_)),
            pl.BlockSpec((None, L, W), lambda b, g, c: (b, c, zb + g)),
            pl.BlockSpec((None, L, LANES), lambda b, g, c: (b, c, 0)),
            pl.BlockSpec((SSD_CONV, W), lambda b, g, c: (0, cxo + g)),
            pl.BlockSpec((SSD_CONV, N), lambda b, g, c: (0, cbo + g)),
            pl.BlockSpec((SSD_CONV, N), lambda b, g, c: (0, cco + g)),
            pl.BlockSpec((1, W), lambda b, g, c: (0, cxo + g)),
            pl.BlockSpec((1, N), lambda b, g, c: (0, cbo + g)),
            pl.BlockSpec((1, N), lambda b, g, c: (0, cco + g)),
            pl.BlockSpec((1, LANES), lambda b, g, c: (0, 0)),
            pl.BlockSpec((GP, LANES, SSD_GROUP_W), lambda b, g, c: (g, 0, 0)),
            pl.BlockSpec((1, W), lambda b, g, c: (0, g)),
            pl.BlockSpec((1, W), lambda b, g, c: (0, g)),
            pl.BlockSpec((1, W), lambda b, g, c: (0, g)),
        ],
        out_specs=pl.BlockSpec((None, L, W), lambda b, g, c: (b, c, g)),
        out_shape=jax.ShapeDtypeStruct((B, S, SSD_INNER), BF16),
        scratch_shapes=[pltpu.VMEM((GP, SSD_STATE, SSD_GROUP_W), F32)],
        compiler_params=_cparams(("parallel", "parallel", "arbitrary")),
        name="ssd_scan",
    )(proj3, proj3, proj3, proj3, proj3, proj3, proj3, dtraw3,
      cw, cw, cw, cbias, cbias, cbias, dtb, jnp.asarray(rg), a_exp, d_exp, ng)


def _merge_kernel(x_ref, a_ref, s_ref, ga_ref, gs_ref, wa_ref, ws_ref, wo_ref, o_ref):
    ya = jnp.dot(a_ref[...], wa_ref[...], preferred_element_type=F32)
    ys = jnp.dot(s_ref[...], ws_ref[...], preferred_element_type=F32)
    mixed = jax.nn.sigmoid(ga_ref[...].astype(F32)) * ya + jax.nn.sigmoid(gs_ref[...].astype(F32)) * ys
    o_ref[...] = x_ref[...] + jnp.dot(mixed.astype(BF16), wo_ref[...], preferred_element_type=F32)


def merge_branches(x2, attn2, ssd2, proj2, wa, ws, wo, tm, row_lo):
    T, D = attn2.shape[0], x2.shape[1]
    off = row_lo // tm
    assert row_lo % tm == 0
    full = lambda a: pl.BlockSpec(a.shape, lambda i: (0, 0))
    return pl.pallas_call(
        _merge_kernel,
        grid=(T // tm,),
        in_specs=[
            pl.BlockSpec((tm, D), lambda i: (i + off, 0)),
            pl.BlockSpec((tm, ATTN_WIDTH), lambda i: (i, 0)),
            pl.BlockSpec((tm, SSD_INNER), lambda i: (i + off, 0)),
            pl.BlockSpec((tm, D), lambda i: (i + off, COL_GA // D)),
            pl.BlockSpec((tm, D), lambda i: (i + off, COL_GS // D)),
            full(wa), full(ws), full(wo),
        ],
        out_specs=pl.BlockSpec((tm, D), lambda i: (i, 0)),
        out_shape=jax.ShapeDtypeStruct((T, D), F32),
        compiler_params=_cparams(("parallel",)),
        name="merge",
    )(x2, attn2, ssd2, proj2, proj2, wa, ws, wo)


def _topk_rows(s, idx, k):
    n = s.shape[0]
    row = lax.broadcasted_iota(I32, s.shape, 0)
    vals, rows, picked = [], [], []
    for _ in range(k):
        m = jnp.max(s, axis=0, keepdims=True)
        first = jnp.min(jnp.where(s == m, row, n), axis=0, keepdims=True)
        hit = row == first
        vals.append(m)
        rows.append(first)
        if idx is not None:
            picked.append(jnp.max(jnp.where(hit, idx, -1), axis=0, keepdims=True))
        s = jnp.where(hit, -jnp.inf, s)
    cat = lambda xs: jnp.concatenate(xs, axis=0)
    return cat(vals), cat(rows), (cat(picked) if idx is not None else None)


def _peer_topk_kernel(q_ref, k1_ref, k2_ref, eidx_ref, gw_ref, *, tt):
    half = PEER_QDIM // 2
    e_rows, g_rows = [], []
    for h in range(PEER_HEADS):
        qa = q_ref[:, h * PEER_QDIM:h * PEER_QDIM + half].astype(BF16)
        qb = q_ref[:, h * PEER_QDIM + half:(h + 1) * PEER_QDIM].astype(BF16)
        s1 = lax.dot_general(k1_ref[h], qa, NT_DIMS, preferred_element_type=F32)
        s2 = lax.dot_general(k2_ref[h], qb, NT_DIMS, preferred_element_type=F32)
        v1, i1, _ = _topk_rows(s1, None, PEER_TOPK)
        v2, i2, _ = _topk_rows(s2, None, PEER_TOPK)
        sub = lax.broadcasted_iota(I32, (SUBLANES, tt), 0)
        cand_parts = [v1[0:1, :] + v2]
        cidx_parts = [i1[0:1, :] * PEER_NKEYS + i2]
        for a in range(1, SUBLANES):
            keep = sub < PEER_TOPK // (a + 1)
            cand_parts.append(jnp.where(keep, v1[a:a + 1, :] + v2[0:SUBLANES, :], -jnp.inf))
            cidx_parts.append(i1[a:a + 1, :] * PEER_NKEYS + i2[0:SUBLANES, :])
        cand_parts.append(v1[SUBLANES:, :] + v2[0:1, :])
        cidx_parts.append(i1[SUBLANES:, :] * PEER_NKEYS + i2[0:1, :])
        cand = jnp.concatenate(cand_parts, axis=0)
        cidx = jnp.concatenate(cidx_parts, axis=0)
        sv, _, ex = _topk_rows(cand, cidx, PEER_TOPK)
        e = jnp.exp(sv - sv[0:1, :])
        g_rows.append(e / jnp.sum(e, axis=0, keepdims=True))
        e_rows.append(ex)
    eidx_ref[...] = jnp.concatenate(e_rows, axis=0).T
    gw = jnp.concatenate(g_rows, axis=0)
    for part in range(tt // LANES):
        gw_ref[part] = gw[:, part * LANES:(part + 1) * LANES]


def peer_topk(q, keys1, keys2, tt):
    T = q.shape[0]
    full3 = lambda a: pl.BlockSpec(a.shape, lambda i: (0, 0, 0))
    return pl.pallas_call(
        functools.partial(_peer_topk_kernel, tt=tt),
        grid=(T // tt,),
        in_specs=[pl.BlockSpec((tt, q.shape[1]), lambda i: (i, 0)), full3(keys1), full3(keys2)],
        out_specs=[
            pl.BlockSpec((tt, PEER_SLOTS), lambda i: (i, 0)),
            pl.BlockSpec((tt // LANES, PEER_SLOTS, LANES), lambda i: (i, 0, 0)),
        ],
        out_shape=[
            jax.ShapeDtypeStruct((T, PEER_SLOTS), I32),
            jax.ShapeDtypeStruct((T // LANES, PEER_SLOTS, LANES), F32),
        ],
        compiler_params=_cparams(("parallel",)),
        name="peer_topk",
    )(q, keys1, keys2)


def pack_table(tab):
    half = tab.shape[1] // 2
    bits = lax.bitcast_convert_type(tab.astype(BF16), jnp.uint16).astype(jnp.uint32)
    return lax.bitcast_convert_type((bits[:, :half] << 16) | bits[:, half:], I32)


def sc_gather_rows(table, idx, win=64):
    V, W = table.shape
    N = idx.shape[0]
    info = plsc.get_sparse_core_info()
    n_cores, n_sub = info.num_cores, info.num_subcores
    workers = n_cores * n_sub
    per_w = N // workers
    steps = per_w // win
    assert steps * win * workers == N and steps % 2 == 0
    mesh = plsc.VectorSubcoreMesh(core_axis_name="c", subcore_axis_name="s")
    dma = pltpu.SemaphoreType.DMA

    @functools.partial(
        pl.kernel, mesh=mesh,
        out_type=jax.ShapeDtypeStruct((N, W), table.dtype),
        scratch_types=[
            pltpu.VMEM((steps, win), I32),
            pltpu.VMEM((win, W), table.dtype),
            pltpu.VMEM((win, W), table.dtype),
            dma, dma, dma, dma,
        ],
    )
    def gather_kernel(table_hbm, idx_hbm, out_hbm, idx_v, rows0, rows1, g0, g1, w0, w1):
        wid = lax.axis_index("s") * n_cores + lax.axis_index("c")
        row0 = wid * steps
        pltpu.sync_copy(idx_hbm.at[pl.ds(row0, steps)], idx_v)
        slots = ((rows0, g0, w0), (rows1, g1, w1))

        def gather(i, slot):
            rows, gsem, _ = slots[slot]
            return pltpu.make_async_copy(table_hbm.at[idx_v.at[i]], rows, gsem)

        def write(i, slot):
            rows, _, wsem = slots[slot]
            off = pl.multiple_of((row0 + i) * win, win)
            return pltpu.make_async_copy(rows, out_hbm.at[pl.ds(off, win)], wsem)

        gather(0, 0).start()

        @pl.loop(0, steps, step=2)
        def _(i):
            @pl.when(i > 0)
            def _():
                write(i - 1, 1).wait()

            gather(i + 1, 1).start()
            gather(i, 0).wait()
            write(i, 0).start()
            write(i, 0).wait()

            @pl.when(i + 2 < steps)
            def _():
                gather(i + 2, 0).start()

            gather(i + 1, 1).wait()
            write(i + 1, 1).start()

        write(steps - 1, 1).wait()

    return gather_kernel(table, idx.reshape(N // win, win))


def _unpack_words(w):
    u = pltpu.bitcast(w, jnp.uint32)
    hi = pltpu.bitcast(u & jnp.uint32(0xFFFF0000), F32)
    lo = pltpu.bitcast(u << 16, F32)
    return hi, lo


def _peer_expert_kernel(x_ref, g_ref, ug_ref, vg_ref, gw_ref, o_ref, *, tt):
    i = pl.program_id(0)
    half = D_MODEL // 2
    x1 = x_ref[...]
    xn = x1 * lax.rsqrt(jnp.mean(x1 * x1, axis=-1, keepdims=True) + EPS) * g_ref[...]
    lane = lax.broadcasted_iota(I32, (PEER_SLOTS, LANES), 1)
    off = (i % (LANES // tt)) * tt
    act = jnp.zeros((PEER_SLOTS, LANES), F32)
    for t in range(tt):
        hi, lo = _unpack_words(ug_ref[t * PEER_SLOTS:(t + 1) * PEER_SLOTS, :])
        prod = hi * xn[t:t + 1, :half] + lo * xn[t:t + 1, half:]
        fold = prod[:, 0:LANES]
        for c in range(1, half // LANES):
            fold = fold + prod[:, c * LANES:(c + 1) * LANES]
        col = jnp.sum(fold, axis=-1, keepdims=True)
        act = jnp.where(lane == off + t, col, act)
    gelu = 0.5 * act * (1.0 + lax.erf(act * (2.0 ** -0.5)))
    hact = gelu * gw_ref[...]
    for t in range(tt):
        hcol = jnp.sum(jnp.where(lane == off + t, hact, 0.0), axis=-1, keepdims=True)
        hi, lo = _unpack_words(vg_ref[t * PEER_SLOTS:(t + 1) * PEER_SLOTS, :])
        o_ref[t:t + 1, :half] = x1[t:t + 1, :half] + jnp.sum(hcol * hi, axis=0, keepdims=True)
        o_ref[t:t + 1, half:] = x1[t:t + 1, half:] + jnp.sum(hcol * lo, axis=0, keepdims=True)


def peer_experts(x1, g2, ug, vg, gw, tt=32):
    T, D = x1.shape
    W = ug.shape[1]
    per = LANES // tt
    return pl.pallas_call(
        functools.partial(_peer_expert_kernel, tt=tt),
        grid=(T // tt,),
        in_specs=[
            pl.BlockSpec((tt, D), lambda i: (i, 0)),
            pl.BlockSpec((1, D), lambda i: (0, 0)),
            pl.BlockSpec((tt * PEER_SLOTS, W), lambda i: (i, 0)),
            pl.BlockSpec((tt * PEER_SLOTS, W), lambda i: (i, 0)),
            pl.BlockSpec((None, PEER_SLOTS, LANES), lambda i: (i // per, 0, 0)),
        ],
        out_specs=pl.BlockSpec((tt, D), lambda i: (i, 0)),
        out_shape=jax.ShapeDtypeStruct((T, D), F32),
        compiler_params=_cparams(("parallel",)),
        name="peer_experts",
    )(x1, g2, ug, vg, gw)


def peer_ffn_residual(x1, norm2_g, wq, keys1, keys2, u_packed, v_packed):
    T = x1.shape[0]
    g2 = norm2_g.astype(F32)[None, :]
    q = norm_matmul(x1, g2, wq, F32, min(1024, T), 1024)
    eidx, gw = peer_topk(q, keys1, keys2, min(256, T))
    flat = eidx.reshape(-1)
    ug = sc_gather_rows(u_packed, flat)
    vg = sc_gather_rows(v_packed, flat)
    return peer_experts(x1, g2, ug, vg, gw)


def kernel(x, norm1_g, w_in, q_norm_g, k_norm_g, conv_w, conv_b, dt_bias, a_log, d_skip, ssd_norm_g,
           w_attn_o, w_ssd_o, w_out, norm2_g, w_peer_q, peer_keys1, peer_keys2, peer_u, peer_v):
    B, S, D = x.shape
    xs = [x[b] for b in range(B)]
    for l in range(norm1_g.shape[0]):
        w = w_in[l]
        dt0 = COL_GA
        w_main = jnp.concatenate([w[:, :dt0], w[:, dt0 + SSD_HEADS:]], axis=1).astype(BF16)
        w_dt = jnp.zeros((D, LANES), BF16).at[:, :SSD_HEADS].set(w[:, dt0:dt0 + SSD_HEADS].astype(BF16))
        g1 = norm1_g[l].astype(F32)[None, :]
        wa, ws, wo = w_attn_o[l].astype(BF16), w_ssd_o[l].astype(BF16), w_out[l].astype(BF16)
        wq = w_peer_q[l].astype(BF16)
        k1, k2 = peer_keys1[l].astype(BF16), peer_keys2[l].astype(BF16)
        up, vp = pack_table(peer_u[l]), pack_table(peer_v[l])
        tm = min(1024, S)
        rows = S // min(PEER_CHUNKS, S // MOBA_Q_TILE) if S >= MOBA_Q_TILE else S
        for b in range(B):
            x2 = xs[b]
            proj = norm_matmul(x2, g1, w_main, BF16, tm, 1024)
            dtraw = norm_matmul(x2, g1, w_dt, F32, tm, LANES)
            proj3 = proj[None]
            yssd = ssd_mixer(proj3, dtraw[None], conv_w[l], conv_b[l], dt_bias[l],
                             a_log[l], d_skip[l], ssd_norm_g[l])
            qaug, kaug, vt = moba_prepare(proj3, q_norm_g[l], k_norm_g[l])
            outs = []
            for lo in range(0, S, rows):
                attn = moba_attention(qaug, kaug, vt, lo, lo + rows)
                x1 = merge_branches(x2, attn[0], yssd[0], proj, wa, ws, wo, min(512, rows), lo)
                outs.append(peer_ffn_residual(x1, norm2_g[l], wq, k1, k2, up, vp))
            xs[b] = jnp.concatenate(outs, axis=0)
    return jnp.stack(xs, axis=0)
```

```python
import functools

import jax
import jax.numpy as jnp
import numpy as np
from jax import lax
from jax.experimental import pallas as pl
from jax.experimental.pallas import tpu as pltpu
from jax.experimental.pallas import tpu_sc as plsc

F32 = jnp.float32
BF16 = jnp.bfloat16
I32 = jnp.int32

EPS = 1e-6
D_MODEL = 1024
N_HEADS = 16
HEAD_DIM = 64
ATTN_WIDTH = N_HEADS * HEAD_DIM
MOBA_BLOCK = 256
MOBA_TOPK = 3
MOBA_KEY_GROUP = 4
MOBA_Q_TILE = 1024
SSD_INNER = 2048
SSD_HEADS = 32
SSD_GROUPS = 8
SSD_STATE = 128
SSD_CONV = 4
SSD_CHUNK = 256
SSD_GN = SSD_GROUPS * SSD_STATE
SSD_GROUP_W = SSD_INNER // SSD_GROUPS
SSD_GROUPS_PER_STEP = 2
PEER_HEADS = 8
PEER_NKEYS = 128
PEER_QDIM = 256
PEER_TOPK = 16
PEER_SLOTS = PEER_HEADS * PEER_TOPK
PEER_CHUNKS = 4
PEER_TOKENS_PER_STEP = 32

LANES = 128
SUBLANES = 8
VMEM_LIMIT = 56 * 1024 * 1024
MASK_BIG = 1e30

COL_Q, COL_K, COL_V = 0, ATTN_WIDTH, 2 * ATTN_WIDTH
COL_Z = 3 * ATTN_WIDTH
COL_X = COL_Z + SSD_INNER
COL_B = COL_X + SSD_INNER
COL_C = COL_B + SSD_GN
COL_GA = COL_C + SSD_GN
COL_GS = COL_GA + D_MODEL
PROJ_COLS = COL_GS + D_MODEL

NT_DIMS = (((1,), (1,)), ((), ()))


def _cparams(sem):
    return pltpu.CompilerParams(dimension_semantics=sem, vmem_limit_bytes=VMEM_LIMIT)


def _norm_matmul_kernel(x_ref, g_ref, w_ref, o_ref, h_ref):
    @pl.when(pl.program_id(1) == 0)
    def _():
        x = x_ref[...]
        y = x * lax.rsqrt(jnp.mean(x * x, axis=-1, keepdims=True) + EPS)
        h_ref[...] = (y * g_ref[...]).astype(h_ref.dtype)

    o_ref[...] = jnp.dot(h_ref[...], w_ref[...], preferred_element_type=F32).astype(o_ref.dtype)


def norm_matmul(x, g, w, out_dtype, tm, tn):
    T, K = x.shape
    N = w.shape[1]
    return pl.pallas_call(
        _norm_matmul_kernel,
        grid=(T // tm, N // tn),
        in_specs=[
            pl.BlockSpec((tm, K), lambda i, j: (i, 0)),
            pl.BlockSpec((1, K), lambda i, j: (0, 0)),
            pl.BlockSpec((K, tn), lambda i, j: (0, j)),
        ],
        out_specs=pl.BlockSpec((tm, tn), lambda i, j: (i, j)),
        out_shape=jax.ShapeDtypeStruct((T, N), out_dtype),
        scratch_shapes=[pltpu.VMEM((tm, K), BF16)],
        compiler_params=_cparams(("parallel", "arbitrary")),
        name="norm_matmul",
    )(x, g, w)


def _split3(v):
    hi = v.astype(BF16).astype(F32)
    r1 = v - hi
    mid = r1.astype(BF16).astype(F32)
    return hi, mid, r1 - mid


def _head_pair_norm(x, g):
    lane = lax.broadcasted_iota(I32, x.shape, 1)
    low = lane < HEAD_DIM
    x2 = x * x
    ss_a = jnp.sum(jnp.where(low, x2, 0.0), axis=-1, keepdims=True)
    ss_b = jnp.sum(jnp.where(low, 0.0, x2), axis=-1, keepdims=True)
    inv = jnp.where(low, lax.rsqrt(ss_a / HEAD_DIM + EPS), lax.rsqrt(ss_b / HEAD_DIM + EPS))
    return x * inv * g


def _kprep_kernel(k_ref, v_ref, g_ref, sl_ref, kaug_ref, kmean_ref, vt_ref, *, tq, tk):
    s_idx = pl.program_id(2)
    for grp in range(tq // tk):
        vblk = v_ref[grp * tk:(grp + 1) * tk, :].astype(F32)
        vt_ref[grp] = vblk.T.astype(BF16)
    kn = _head_pair_norm(k_ref[...].astype(F32), g_ref[...])
    nb = tq // MOBA_BLOCK
    km = jnp.mean(kn.reshape(nb, MOBA_BLOCK, LANES), axis=1)
    lane = lax.broadcasted_iota(I32, (tq, LANES), 1)
    row = lax.broadcasted_iota(I32, (tq, LANES), 0) + s_idx * tq
    blk = row // MOBA_BLOCK
    pos = row.astype(F32)
    lane_m = lax.broadcasted_iota(I32, (nb, LANES), 1)
    heads = ((kn, km), (pltpu.roll(kn, HEAD_DIM, axis=1), pltpu.roll(km, HEAD_DIM, axis=1)))
    for hh, (kk, kmm) in enumerate(heads):
        hi, mid, lo = _split3(sl_ref[hh:hh + 1, :] * pos)
        aug = jnp.where(lane < HEAD_DIM, kk, 0.0)
        aug = jnp.where((lane >= 64) & (lane < 96), (lane - 64 == blk).astype(F32), aug)
        aug = jnp.where(lane == 96, hi, aug)
        aug = jnp.where(lane == 97, mid, aug)
        aug = jnp.where(lane == 98, lo, aug)
        aug = jnp.where((lane >= 99) & (lane < 102), 1.0, aug)
        kaug_ref[hh] = aug.astype(BF16)
        kmean_ref[hh] = jnp.where(lane_m < HEAD_DIM, kmm, 0.0)


def _qprep_kernel(q_ref, g_ref, sl_ref, kmean_ref, qaug_ref, *, tq, nblk):
    s_idx = pl.program_id(2)
    qn = _head_pair_norm(q_ref[...].astype(F32), g_ref[...])
    lane = lax.broadcasted_iota(I32, (tq, LANES), 1)
    t = (lax.broadcasted_iota(I32, (tq, LANES), 0) + s_idx * tq).astype(F32)
    jrow = lax.broadcasted_iota(I32, (32, tq), 0)
    own = (lax.broadcasted_iota(I32, (32, tq), 1) + s_idx * tq) // MOBA_BLOCK
    heads = (qn, pltpu.roll(qn, HEAD_DIM, axis=1))
    for hh, qh in enumerate(heads):
        qq = jnp.where(lane < HEAD_DIM, qh, 0.0)
        km_rows = jnp.concatenate([kmean_ref[hh], jnp.zeros((32 - nblk, LANES), F32)], axis=0) \
            if nblk < 32 else kmean_ref[hh]
        g = lax.dot_general(km_rows, qq, NT_DIMS, precision=lax.Precision.HIGHEST,
                            preferred_element_type=F32)
        g = jnp.where(jrow < own, g, -jnp.inf)
        allowed = jrow == own
        for r in range(MOBA_TOPK):
            m = jnp.max(g, axis=0, keepdims=True)
            first = jnp.min(jnp.where(g == m, jrow, 1 << 20), axis=0, keepdims=True)
            hit = jrow == first
            allowed = allowed | (hit & (own > r))
            g = jnp.where(hit, -jnp.inf, g)
        mask_t = jnp.where(allowed, 0.0, -MASK_BIG)
        mask = jnp.concatenate([jnp.zeros((64, tq), F32), mask_t, jnp.zeros((32, tq), F32)], axis=0).T
        hi, mid, lo = _split3(-sl_ref[hh:hh + 1, :] * t)
        aug = qq * (HEAD_DIM ** -0.5)
        aug = jnp.where((lane >= 64) & (lane < 96), mask, aug)
        aug = jnp.where((lane >= 96) & (lane < 99), 1.0, aug)
        aug = jnp.where(lane == 99, hi, aug)
        aug = jnp.where(lane == 100, mid, aug)
        aug = jnp.where(lane == 101, lo, aug)
        qaug_ref[hh] = aug.astype(BF16)


def _attn_kernel(q_ref, k_ref, vt_ref, o_ref, acc_ref, m_ref, *, tq, tk, q_off):
    i = pl.program_id(2) + q_off
    vrow = lax.broadcasted_iota(I32, (LANES, tk), 0)
    low = vrow < HEAD_DIM
    last = (i * tq) // tk

    def group(g, diag):
        start = pl.multiple_of(g * tk, tk)
        vt = vt_ref[g]
        one = jnp.ones_like(vt)
        if diag:
            kpos = lax.broadcasted_iota(I32, (tk, tq), 0) + g * tk
            qpos = lax.broadcasted_iota(I32, (tk, tq), 1) + i * tq
            causal = kpos <= qpos
        for hh in range(2):
            kj = k_ref[hh, pl.ds(start, tk), :]
            s = lax.dot_general(kj, q_ref[hh], NT_DIMS, preferred_element_type=F32)
            if diag:
                s = jnp.where(causal, s, -MASK_BIG)
            m_old = m_ref[hh]
            m_new = jnp.maximum(m_old, jnp.max(s, axis=0, keepdims=True))
            alpha = jnp.exp(m_old - m_new)
            p = jnp.exp(s - m_new).astype(BF16)
            vaug = jnp.where(low, vt, one) if hh == 0 else jnp.where(low, one, vt)
            acc_ref[hh] = alpha * acc_ref[hh] + jnp.dot(vaug, p, preferred_element_type=F32)
            m_ref[hh] = m_new

    acc_ref[...] = jnp.zeros_like(acc_ref)
    m_ref[...] = jnp.full_like(m_ref, -MASK_BIG)
    group(last, True)

    def body(g, carry):
        group(g, False)
        return carry

    lax.fori_loop(0, last, body, 0)
    a = acc_ref[0]
    b = acc_ref[1]
    low_q = lax.broadcasted_iota(I32, (LANES, tq), 0) < HEAD_DIM
    out = jnp.where(low_q, a / a[HEAD_DIM:HEAD_DIM + 1, :], b / b[0:1, :])
    o_ref[...] = out.T.astype(o_ref.dtype)


def moba_prepare(proj3, q_norm_g, k_norm_g):
    B, S, _ = proj3.shape
    nblk = S // MOBA_BLOCK
    assert S % MOBA_BLOCK == 0 and nblk <= 32
    HP = N_HEADS // 2
    tq = min(2048, S)
    slopes = jnp.exp2(-8.0 * jnp.arange(1, N_HEADS + 1, dtype=F32) / N_HEADS)
    sl = jnp.zeros((HP, SUBLANES, LANES), F32)
    sl = sl.at[:, 0, :].set(slopes[0::2, None]).at[:, 1, :].set(slopes[1::2, None])
    gq = jnp.tile(q_norm_g.astype(F32), 2)[None, :]
    gk = jnp.tile(k_norm_g.astype(F32), 2)[None, :]
    qb, kb, vb = COL_Q // LANES, COL_K // LANES, COL_V // LANES
    grid = (B, HP, S // tq)
    sem3 = ("parallel", "parallel", "parallel")

    nbt = tq // MOBA_BLOCK
    tk = MOBA_KEY_GROUP * MOBA_BLOCK
    assert tq % tk == 0
    kaug, kmean, vt = pl.pallas_call(
        functools.partial(_kprep_kernel, tq=tq, tk=tk),
        grid=grid,
        in_specs=[
            pl.BlockSpec((None, tq, LANES), lambda b, p, s: (b, s, kb + p)),
            pl.BlockSpec((None, tq, LANES), lambda b, p, s: (b, s, vb + p)),
            pl.BlockSpec((1, LANES), lambda b, p, s: (0, 0)),
            pl.BlockSpec((None, SUBLANES, LANES), lambda b, p, s: (p, 0, 0)),
        ],
        out_specs=[
            pl.BlockSpec((None, 2, tq, LANES), lambda b, p, s: (b, p, s, 0)),
            pl.BlockSpec((None, 2, nbt, LANES), lambda b, p, s: (b, p, s, 0)),
            pl.BlockSpec((None, None, tq // tk, LANES, tk), lambda b, p, s: (b, p, s, 0, 0)),
        ],
        out_shape=[
            jax.ShapeDtypeStruct((B, N_HEADS, S, LANES), BF16),
            jax.ShapeDtypeStruct((B, N_HEADS, nblk, LANES), F32),
            jax.ShapeDtypeStruct((B, HP, S // tk, LANES, tk), BF16),
        ],
        compiler_params=_cparams(sem3),
        name="moba_kprep",
    )(proj3, proj3, gk, sl)

    qaug = pl.pallas_call(
        functools.partial(_qprep_kernel, tq=tq, nblk=nblk),
        grid=grid,
        in_specs=[
            pl.BlockSpec((None, tq, LANES), lambda b, p, s: (b, s, qb + p)),
            pl.BlockSpec((1, LANES), lambda b, p, s: (0, 0)),
            pl.BlockSpec((None, SUBLANES, LANES), lambda b, p, s: (p, 0, 0)),
            pl.BlockSpec((None, 2, nblk, LANES), lambda b, p, s: (b, p, 0, 0)),
        ],
        out_specs=pl.BlockSpec((None, 2, tq, LANES), lambda b, p, s: (b, p, s, 0)),
        out_shape=jax.ShapeDtypeStruct((B, N_HEADS, S, LANES), BF16),
        compiler_params=_cparams(sem3),
        name="moba_qprep",
    )(proj3, gq, sl, kmean)

    return qaug, kaug, vt


def moba_attention(qaug, kaug, vt, row_lo, row_hi):
    B, _, S, _ = qaug.shape
    HP = N_HEADS // 2
    tk = MOBA_KEY_GROUP * MOBA_BLOCK
    ta = min(MOBA_Q_TILE, row_hi - row_lo)
    assert tk % ta == 0 and row_lo % ta == 0 and (row_hi - row_lo) % ta == 0
    q_off = row_lo // ta
    return pl.pallas_call(
        functools.partial(_attn_kernel, tq=ta, tk=tk, q_off=q_off),
        grid=(B, HP, (row_hi - row_lo) // ta),
        in_specs=[
            pl.BlockSpec((None, 2, ta, LANES), lambda b, p, i: (b, p, i + q_off, 0)),
            pl.BlockSpec((None, 2, S, LANES), lambda b, p, i: (b, p, 0, 0)),
            pl.BlockSpec((None, None, S // tk, LANES, tk), lambda b, p, i: (b, p, 0, 0, 0)),
        ],
        out_specs=pl.BlockSpec((None, ta, LANES), lambda b, p, i: (b, i, p)),
        out_shape=jax.ShapeDtypeStruct((B, row_hi - row_lo, ATTN_WIDTH), BF16),
        scratch_shapes=[pltpu.VMEM((2, LANES, ta), F32), pltpu.VMEM((2, 1, ta), F32)],
        compiler_params=_cparams(("parallel", "parallel", "arbitrary")),
        name="moba_attn",
    )(qaug, kaug, vt)


def _silu(x):
    return x * (1.0 / (1.0 + jnp.exp(-x)))


def _conv_silu(u_ref, halo_ref, w_ref, b_ref, first):
    u = u_ref[...].astype(F32)
    halo = jnp.where(first, 0.0, halo_ref[...].astype(F32))
    ext = jnp.concatenate([halo, u], axis=0)
    w = w_ref[...]
    out = b_ref[...] + w[3:4, :] * u
    for back in range(1, SSD_CONV):
        out = out + w[3 - back:4 - back, :] * pltpu.roll(ext, back, axis=0)[SUBLANES:, :]
    return _silu(out)


def _ssd_group(xs, bm, cm, z, dt, rg, a, d, ng, state_ref):
    L, W = xs.shape
    hp = lax.Precision.HIGHEST
    dtx = jnp.dot(dt, rg, precision=hp, preferred_element_type=F32)
    ax = dtx * a
    rr = lax.broadcasted_iota(I32, (L, L), 0)
    cc = lax.broadcasted_iota(I32, (L, L), 1)
    causal = cc <= rr
    acum = jnp.dot(causal.astype(F32), ax, precision=hp, preferred_element_type=F32)
    acum_t = acum.T
    a_last = acum[L - 1:L, :]
    xdt = xs * dtx
    cmb = cm.astype(BF16)
    cb = lax.dot_general(cmb, bm.astype(BF16), NT_DIMS, preferred_element_type=F32)
    lane = lax.broadcasted_iota(I32, (L, W), 1)
    y = jnp.zeros((L, W), F32)
    for r in range(W // 64):
        seg = acum[:, 64 * r:64 * r + 1] - acum_t[64 * r:64 * r + 1, :]
        lmat = jnp.exp(jnp.where(causal, seg, -jnp.inf))
        xr = jnp.where((lane >= 64 * r) & (lane < 64 * r + 64), xdt, 0.0).astype(BF16)
        y = y + jnp.dot((cb * lmat).astype(BF16), xr, preferred_element_type=F32)
    state = state_ref[...]
    y = y + jnp.dot(cmb, state.astype(BF16), preferred_element_type=F32) * jnp.exp(acum)
    wgt = (xdt * jnp.exp(a_last - acum)).astype(BF16)
    state_ref[...] = state * jnp.exp(a_last) + jnp.dot(bm.T.astype(BF16), wgt, preferred_element_type=F32)
    y = y + d * xs
    y = y * _silu(z)
    y = y * lax.rsqrt(jnp.mean(y * y, axis=-1, keepdims=True) + EPS)
    return y * ng


def _ssd_kernel(x_ref, xh_ref, b_ref, bh_ref, c_ref, ch_ref, z_ref, dt_ref,
                wx_ref, wb_ref, wc_ref, bx_ref, bb_ref, bc_ref, dtb_ref, rg_ref,
                a_ref, d_ref, ng_ref, y_ref, state_ref):
    first = pl.program_id(2) == 0
    W, N = SSD_GROUP_W, SSD_STATE

    @pl.when(first)
    def _():
        state_ref[...] = jnp.zeros_like(state_ref)

    xs = _conv_silu(x_ref, xh_ref, wx_ref, bx_ref, first)
    bm = _conv_silu(b_ref, bh_ref, wb_ref, bb_ref, first)
    cm = _conv_silu(c_ref, ch_ref, wc_ref, bc_ref, first)
    dt = jax.nn.softplus(dt_ref[...] + dtb_ref[...])
    z = z_ref[...].astype(F32)
    for gi in range(SSD_GROUPS_PER_STEP):
        cw, cn = slice(gi * W, (gi + 1) * W), slice(gi * N, (gi + 1) * N)
        y = _ssd_group(xs[:, cw], bm[:, cn], cm[:, cn], z[:, cw], dt, rg_ref[gi],
                       a_ref[:, cw], d_ref[:, cw], ng_ref[:, cw], state_ref.at[gi])
        y_ref[:, cw] = y.astype(y_ref.dtype)


def ssd_mixer(proj3, dtraw3, conv_w, conv_b, dt_bias, a_log, d_skip, norm_g):
    B, S, _ = proj3.shape
    GP = SSD_GROUPS_PER_STEP
    L, W, N = SSD_CHUNK, SSD_GROUP_W * GP, SSD_STATE * GP
    assert S % L == 0 and SSD_GROUPS % GP == 0
    nc = S // L
    hb = L // SUBLANES
    xb, bb, cb_, zb = COL_X // W, COL_B // N, COL_C // N, COL_Z // W
    rep = SSD_INNER // SSD_HEADS
    a_exp = jnp.repeat(-jnp.exp(a_log.astype(F32)), rep)[None, :]
    d_exp = jnp.repeat(d_skip.astype(F32), rep)[None, :]
    ng = norm_g.astype(F32)[None, :]
    dtb = jnp.zeros((1, LANES), F32).at[0, :SSD_HEADS].set(dt_bias.astype(F32))
    head_of_chan = np.arange(SSD_INNER) // rep
    rg = (np.arange(LANES)[None, :, None]
          == head_of_chan.reshape(SSD_GROUPS, 1, SSD_GROUP_W)).astype(np.float32)
    cw = conv_w.astype(F32)
    cbias = conv_b.astype(F32)[None, :]
    cxo, cbo, cco = 0, SSD_INNER // N, (SSD_INNER + SSD_GN) // N

    def halo(col):
        return lambda b, g, c: (b, jnp.maximum(c * hb - 1, 0), col + g)

    return pl.pallas_call(
        _ssd_kernel,
        grid=(B, SSD_GROUPS // GP, nc),
        in_specs=[
            pl.BlockSpec((None, L, W), lambda b, g, c: (b, c, xb + g)),
            pl.BlockSpec((None, SUBLANES, W), halo(xb)),
            pl.BlockSpec((None, L, N), lambda b, g, c: (b, c, bb + g)),
            pl.BlockSpec((None, SUBLANES, N), halo(bb)),
            pl.BlockSpec((None, L, N), lambda b, g, c: (b, c, cb_ + g)),
            pl.BlockSpec((None, SUBLANES, N), halo(cb_)),
            pl.BlockSpec((None, L, W), lambda b, g, c: (b, c, zb + g)),
            pl.BlockSpec((None, L, LANES), lambda b, g, c: (b, c, 0)),
            pl.BlockSpec((SSD_CONV, W), lambda b, g, c: (0, cxo + g)),
            pl.BlockSpec((SSD_CONV, N), lambda b, g, c: (0, cbo + g)),
            pl.BlockSpec((SSD_CONV, N), lambda b, g, c: (0, cco + g)),
            pl.BlockSpec((1, W), lambda b, g, c: (0, cxo + g)),
            pl.BlockSpec((1, N), lambda b, g, c: (0, cbo + g)),
            pl.BlockSpec((1, N), lambda b, g, c: (0, cco + g)),
            pl.BlockSpec((1, LANES), lambda b, g, c: (0, 0)),
            pl.BlockSpec((GP, LANES, SSD_GROUP_W), lambda b, g, c: (g, 0, 0)),
            pl.BlockSpec((1, W), lambda b, g, c: (0, g)),
            pl.BlockSpec((1, W), lambda b, g, c: (0, g)),
            pl.BlockSpec((1, W), lambda b, g, c: (0, g)),
        ],
        out_specs=pl.BlockSpec((None, L, W), lambda b, g, c: (b, c, g)),
        out_shape=jax.ShapeDtypeStruct((B, S, SSD_INNER), BF16),
        scratch_shapes=[pltpu.VMEM((GP, SSD_STATE, SSD_GROUP_W), F32)],
        compiler_params=_cparams(("parallel", "parallel", "arbitrary")),
        name="ssd_scan",
    )(proj3, proj3, proj3, proj3, proj3, proj3, proj3, dtraw3,
      cw, cw, cw, cbias, cbias, cbias, dtb, jnp.asarray(rg), a_exp, d_exp, ng)


def _merge_kernel(x_ref, a_ref, s_ref, ga_ref, gs_ref, wa_ref, ws_ref, wo_ref, o_ref):
    ya = jnp.dot(a_ref[...], wa_ref[...], preferred_element_type=F32)
    ys = jnp.dot(s_ref[...], ws_ref[...], preferred_element_type=F32)
    mixed = jax.nn.sigmoid(ga_ref[...].astype(F32)) * ya + jax.nn.sigmoid(gs_ref[...].astype(F32)) * ys
    o_ref[...] = x_ref[...] + jnp.dot(mixed.astype(BF16), wo_ref[...], preferred_element_type=F32)


def merge_branches(x2, attn2, ssd2, proj2, wa, ws, wo, tm, row_lo):
    T, D = attn2.shape[0], x2.shape[1]
    off = row_lo // tm
    assert row_lo % tm == 0
    full = lambda a: pl.BlockSpec(a.shape, lambda i: (0, 0))
    return pl.pallas_call(
        _merge_kernel,
        grid=(T // tm,),
        in_specs=[
            pl.BlockSpec((tm, D), lambda i: (i + off, 0)),
            pl.BlockSpec((tm, ATTN_WIDTH), lambda i: (i, 0)),
            pl.BlockSpec((tm, SSD_INNER), lambda i: (i + off, 0)),
            pl.BlockSpec((tm, D), lambda i: (i + off, COL_GA // D)),
            pl.BlockSpec((tm, D), lambda i: (i + off, COL_GS // D)),
            full(wa), full(ws), full(wo),
        ],
        out_specs=pl.BlockSpec((tm, D), lambda i: (i, 0)),
        out_shape=jax.ShapeDtypeStruct((T, D), F32),
        compiler_params=_cparams(("parallel",)),
        name="merge",
    )(x2, attn2, ssd2, proj2, proj2, wa, ws, wo)


def _topk_rows(s, idx, k):
    n = s.shape[0]
    row = lax.broadcasted_iota(I32, s.shape, 0)
    vals, rows, picked = [], [], []
    for _ in range(k):
        m = jnp.max(s, axis=0, keepdims=True)
        first = jnp.min(jnp.where(s == m, row, n), axis=0, keepdims=True)
        hit = row == first
        vals.append(m)
        rows.append(first)
        if idx is not None:
            picked.append(jnp.max(jnp.where(hit, idx, -1), axis=0, keepdims=True))
        s = jnp.where(hit, -jnp.inf, s)
    cat = lambda xs: jnp.concatenate(xs, axis=0)
    return cat(vals), cat(rows), (cat(picked) if idx is not None else None)


def _peer_topk_kernel(q_ref, k1_ref, k2_ref, eidx_ref, gw_ref, *, tt):
    half = PEER_QDIM // 2
    e_rows, g_rows = [], []
    for h in range(PEER_HEADS):
        qa = q_ref[:, h * PEER_QDIM:h * PEER_QDIM + half].astype(BF16)
        qb = q_ref[:, h * PEER_QDIM + half:(h + 1) * PEER_QDIM].astype(BF16)
        s1 = lax.dot_general(k1_ref[h], qa, NT_DIMS, preferred_element_type=F32)
        s2 = lax.dot_general(k2_ref[h], qb, NT_DIMS, preferred_element_type=F32)
        v1, i1, _ = _topk_rows(s1, None, PEER_TOPK)
        v2, i2, _ = _topk_rows(s2, None, PEER_TOPK)
        sub = lax.broadcasted_iota(I32, (SUBLANES, tt), 0)
        cand_parts = [v1[0:1, :] + v2]
        cidx_parts = [i1[0:1, :] * PEER_NKEYS + i2]
        for a in range(1, SUBLANES):
            keep = sub < PEER_TOPK // (a + 1)
            cand_parts.append(jnp.where(keep, v1[a:a + 1, :] + v2[0:SUBLANES, :], -jnp.inf))
            cidx_parts.append(i1[a:a + 1, :] * PEER_NKEYS + i2[0:SUBLANES, :])
        cand_parts.append(v1[SUBLANES:, :] + v2[0:1, :])
        cidx_parts.append(i1[SUBLANES:, :] * PEER_NKEYS + i2[0:1, :])
        cand = jnp.concatenate(cand_parts, axis=0)
        cidx = jnp.concatenate(cidx_parts, axis=0)
        sv, _, ex = _topk_rows(cand, cidx, PEER_TOPK)
        e = jnp.exp(sv - sv[0:1, :])
        g_rows.append(e / jnp.sum(e, axis=0, keepdims=True))
        e_rows.append(ex)
    eidx_ref[...] = jnp.concatenate(e_rows, axis=0).T
    gw_ref[...] = jnp.concatenate(g_rows, axis=0).T


def peer_topk(q, keys1, keys2, tt):
    T = q.shape[0]
    full3 = lambda a: pl.BlockSpec(a.shape, lambda i: (0, 0, 0))
    return pl.pallas_call(
        functools.partial(_peer_topk_kernel, tt=tt),
        grid=(T // tt,),
        in_specs=[pl.BlockSpec((tt, q.shape[1]), lambda i: (i, 0)), full3(keys1), full3(keys2)],
        out_specs=[
            pl.BlockSpec((tt, PEER_SLOTS), lambda i: (i, 0)),
            pl.BlockSpec((tt, PEER_SLOTS), lambda i: (i, 0)),
        ],
        out_shape=[
            jax.ShapeDtypeStruct((T, PEER_SLOTS), I32),
            jax.ShapeDtypeStruct((T, PEER_SLOTS), F32),
        ],
        compiler_params=_cparams(("parallel",)),
        name="peer_topk",
    )(q, keys1, keys2)


def pack_table(tab):
    half = tab.shape[1] // 2
    bits = lax.bitcast_convert_type(tab.astype(BF16), jnp.uint16).astype(jnp.uint32)
    return lax.bitcast_convert_type((bits[:, :half] << 16) | bits[:, half:], I32)


def sc_gather_rows(table, idx, win=64):
    V, W = table.shape
    N = idx.shape[0]
    info = plsc.get_sparse_core_info()
    n_cores, n_sub = info.num_cores, info.num_subcores
    workers = n_cores * n_sub
    per_w = N // workers
    steps = per_w // win
    assert steps * win * workers == N and steps % 2 == 0
    mesh = plsc.VectorSubcoreMesh(core_axis_name="c", subcore_axis_name="s")
    dma = pltpu.SemaphoreType.DMA

    @functools.partial(
        pl.kernel, mesh=mesh,
        out_type=jax.ShapeDtypeStruct((N, W), table.dtype),
        scratch_types=[
            pltpu.VMEM((steps, win), I32),
            pltpu.VMEM((win, W), table.dtype),
            pltpu.VMEM((win, W), table.dtype),
            dma, dma, dma, dma,
        ],
    )
    def gather_kernel(table_hbm, idx_hbm, out_hbm, idx_v, rows0, rows1, g0, g1, w0, w1):
        wid = lax.axis_index("s") * n_cores + lax.axis_index("c")
        row0 = wid * steps
        pltpu.sync_copy(idx_hbm.at[pl.ds(row0, steps)], idx_v)
        slots = ((rows0, g0, w0), (rows1, g1, w1))

        def gather(i, slot):
            rows, gsem, _ = slots[slot]
            return pltpu.make_async_copy(table_hbm.at[idx_v.at[i]], rows, gsem)

        def write(i, slot):
            rows, _, wsem = slots[slot]
            off = pl.multiple_of((row0 + i) * win, win)
            return pltpu.make_async_copy(rows, out_hbm.at[pl.ds(off, win)], wsem)

        gather(0, 0).start()

        @pl.loop(0, steps, step=2)
        def _(i):
            @pl.when(i > 0)
            def _():
                write(i - 1, 1).wait()

            gather(i + 1, 1).start()
            gather(i, 0).wait()
            write(i, 0).start()
            write(i, 0).wait()

            @pl.when(i + 2 < steps)
            def _():
                gather(i + 2, 0).start()

            gather(i + 1, 1).wait()
            write(i + 1, 1).start()

        write(steps - 1, 1).wait()

    return gather_kernel(table, idx.reshape(N // win, win))


def sc_expert_scores(table, idx, xn, win=64):
    V, W = table.shape
    N = idx.shape[0]
    T, D = xn.shape
    slots = N // T
    info = plsc.get_sparse_core_info()
    n_cores, n_sub, L = info.num_cores, info.num_subcores, info.num_lanes
    workers = n_cores * n_sub
    tok_w = T // workers
    halves = slots // win
    assert tok_w * workers == T and halves == 2 and D == 2 * W and W % L == 0 and win % 32 == 0
    mesh = plsc.VectorSubcoreMesh(core_axis_name="c", subcore_axis_name="s")
    dma = pltpu.SemaphoreType.DMA
    RB = 32

    @functools.partial(
        pl.kernel, mesh=mesh,
        out_type=jax.ShapeDtypeStruct((N,), F32),
        scratch_types=[pltpu.VMEM((tok_w * halves, win), I32),
                       pltpu.VMEM((win, W), I32), pltpu.VMEM((win, W), I32),
                       pltpu.VMEM((D,), F32), pltpu.VMEM((slots,), F32), dma, dma],
        compiler_params=pltpu.CompilerParams(needs_layout_passes=False),
    )
    def k(table_hbm, idx_hbm, xn_hbm, act_hbm, idx_v, rows0, rows1, x_v, act_v, g0, g1):
        wid = lax.axis_index("s") * n_cores + lax.axis_index("c")
        tok0 = wid * tok_w
        pltpu.sync_copy(idx_hbm.at[pl.ds(tok0 * halves, tok_w * halves)], idx_v)
        bufs = ((rows0, g0), (rows1, g1))

        def gather(step, slot):
            rows, sem = bufs[slot]
            return pltpu.make_async_copy(table_hbm.at[idx_v.at[step]], rows, sem)

        lane = lax.iota(I32, L)

        def compute(rows, base):
            for rg in range(win // RB):
                def body(kk, accs):
                    k16 = pl.multiple_of(kk * L, L)
                    xh = x_v[pl.ds(k16, L)]
                    xl = x_v[pl.ds(W + k16, L)]
                    out = []
                    for r in range(RB):
                        w = rows[rg * RB + r, pl.ds(k16, L)]
                        hi = lax.bitcast_convert_type(w & jnp.int32(-65536), F32)
                        lo = lax.bitcast_convert_type(w << 16, F32)
                        out.append(accs[r] + hi * xh + lo * xl)
                    return tuple(out)

                accs = lax.fori_loop(0, W // L, body, tuple(jnp.zeros((L,), F32) for _ in range(RB)))
                for j in range(RB // L):
                    res = jnp.zeros((L,), F32)
                    for r in range(L):
                        res = jnp.where(lane == r, jnp.sum(accs[j * L + r]), res)
                    act_v[pl.ds(base + rg * RB + j * L, L)] = res

        gather(0, 0).start()

        @pl.loop(0, tok_w)
        def _(t):
            pltpu.sync_copy(xn_hbm.at[tok0 + t], x_v)
            gather(2 * t + 1, 1).start()
            gather(2 * t, 0).wait()
            compute(rows0, 0)

            @pl.when(t + 1 < tok_w)
            def _():
                gather(2 * t + 2, 0).start()

            gather(2 * t + 1, 1).wait()
            compute(rows1, win)
            pltpu.sync_copy(act_v, act_hbm.at[pl.ds(pl.multiple_of((tok0 + t) * slots, slots), slots)])

    return k(table, idx.reshape(N // win, win), xn)


def _unpack_words(w):
    u = pltpu.bitcast(w, jnp.uint32)
    hi = pltpu.bitcast(u & jnp.uint32(0xFFFF0000), F32)
    lo = pltpu.bitcast(u << 16, F32)
    return hi, lo


def _rms_scale_kernel(x_ref, g_ref, o_ref):
    x = x_ref[...]
    o_ref[...] = x * lax.rsqrt(jnp.mean(x * x, axis=-1, keepdims=True) + EPS) * g_ref[...]


def rms_scale(x, g, tm):
    T, D = x.shape
    return pl.pallas_call(
        _rms_scale_kernel,
        grid=(T // tm,),
        in_specs=[pl.BlockSpec((tm, D), lambda i: (i, 0)), pl.BlockSpec((1, D), lambda i: (0, 0))],
        out_specs=pl.BlockSpec((tm, D), lambda i: (i, 0)),
        out_shape=jax.ShapeDtypeStruct((T, D), F32),
        compiler_params=_cparams(("parallel",)),
        name="rms_scale",
    )(x, g)


def _peer_expert_kernel(x_ref, act_ref, gw_ref, vg_ref, o_ref, *, tt):
    half = D_MODEL // 2
    x1 = x_ref[...]
    act = act_ref[...]
    gelu = 0.5 * act * (1.0 + lax.erf(act * (2.0 ** -0.5)))
    hact = gelu * gw_ref[...]
    hact_t = jnp.concatenate([hact, jnp.zeros((LANES - tt, PEER_SLOTS), F32)], axis=0).T
    lane = lax.broadcasted_iota(I32, (PEER_SLOTS, LANES), 1)
    for t in range(tt):
        hcol = jnp.sum(jnp.where(lane == t, hact_t, 0.0), axis=-1, keepdims=True)
        hi, lo = _unpack_words(vg_ref[t * PEER_SLOTS:(t + 1) * PEER_SLOTS, :])
        o_ref[t:t + 1, :half] = x1[t:t + 1, :half] + jnp.sum(hcol * hi, axis=0, keepdims=True)
        o_ref[t:t + 1, half:] = x1[t:t + 1, half:] + jnp.sum(hcol * lo, axis=0, keepdims=True)


def peer_experts(x1, act, gw, vg, tt=PEER_TOKENS_PER_STEP):
    T, D = x1.shape
    W = vg.shape[1]
    return pl.pallas_call(
        functools.partial(_peer_expert_kernel, tt=tt),
        grid=(T // tt,),
        in_specs=[
            pl.BlockSpec((tt, D), lambda i: (i, 0)),
            pl.BlockSpec((tt, PEER_SLOTS), lambda i: (i, 0)),
            pl.BlockSpec((tt, PEER_SLOTS), lambda i: (i, 0)),
            pl.BlockSpec((tt * PEER_SLOTS, W), lambda i: (i, 0)),
        ],
        out_specs=pl.BlockSpec((tt, D), lambda i: (i, 0)),
        out_shape=jax.ShapeDtypeStruct((T, D), F32),
        compiler_params=_cparams(("parallel",)),
        name="peer_experts",
    )(x1, act, gw, vg)


def peer_ffn_residual(x1, norm2_g, wq, keys1, keys2, u_packed, v_packed):
    T = x1.shape[0]
    g2 = norm2_g.astype(F32)[None, :]
    q = norm_matmul(x1, g2, wq, F32, min(1024, T), 1024)
    eidx, gw = peer_topk(q, keys1, keys2, min(256, T))
    flat = eidx.reshape(-1)
    xn = rms_scale(x1, g2, min(512, T))
    act = sc_expert_scores(u_packed, flat, xn).reshape(T, PEER_SLOTS)
    vg = sc_gather_rows(v_packed, flat)
    return peer_experts(x1, act, gw, vg)


def kernel(x, norm1_g, w_in, q_norm_g, k_norm_g, conv_w, conv_b, dt_bias, a_log, d_skip, ssd_norm_g,
           w_attn_o, w_ssd_o, w_out, norm2_g, w_peer_q, peer_keys1, peer_keys2, peer_u, peer_v):
    B, S, D = x.shape
    xs = [x[b] for b in range(B)]
    for l in range(norm1_g.shape[0]):
        w = w_in[l]
        dt0 = COL_GA
        w_main = jnp.concatenate([w[:, :dt0], w[:, dt0 + SSD_HEADS:]], axis=1).astype(BF16)
        w_dt = jnp.zeros((D, LANES), BF16).at[:, :SSD_HEADS].set(w[:, dt0:dt0 + SSD_HEADS].astype(BF16))
        g1 = norm1_g[l].astype(F32)[None, :]
        wa, ws, wo = w_attn_o[l].astype(BF16), w_ssd_o[l].astype(BF16), w_out[l].astype(BF16)
        wq = w_peer_q[l].astype(BF16)
        k1, k2 = peer_keys1[l].astype(BF16), peer_keys2[l].astype(BF16)
        up, vp = pack_table(peer_u[l]), pack_table(peer_v[l])
        tm = min(1024, S)
        rows = S // min(PEER_CHUNKS, S // MOBA_Q_TILE) if S >= MOBA_Q_TILE else S
        for b in range(B):
            x2 = xs[b]
            proj = norm_matmul(x2, g1, w_main, BF16, tm, 1024)
            dtraw = norm_matmul(x2, g1, w_dt, F32, tm, LANES)
            proj3 = proj[None]
            yssd = ssd_mixer(proj3, dtraw[None], conv_w[l], conv_b[l], dt_bias[l],
                             a_log[l], d_skip[l], ssd_norm_g[l])
            qaug, kaug, vt = moba_prepare(proj3, q_norm_g[l], k_norm_g[l])
            outs = []
            for lo in range(0, S, rows):
                attn = moba_attention(qaug, kaug, vt, lo, lo + rows)
                x1 = merge_branches(x2, attn[0], yssd[0], proj, wa, ws, wo, min(512, rows), lo)
                outs.append(peer_ffn_residual(x1, norm2_g[l], wq, k1, k2, up, vp))
            xs[b] = jnp.concatenate(outs, axis=0)
    return jnp.stack(xs, axis=0)
```

```python
import functools

import jax
import jax.numpy as jnp
import numpy as np
from jax import lax
from jax.experimental import pallas as pl
from jax.experimental.pallas import tpu as pltpu
from jax.experimental.pallas import tpu_sc as plsc

F32 = jnp.float32
BF16 = jnp.bfloat16
I32 = jnp.int32

EPS = 1e-6
D_MODEL = 1024
N_HEADS = 16
HEAD_DIM = 64
ATTN_WIDTH = N_HEADS * HEAD_DIM
MOBA_BLOCK = 256
MOBA_TOPK = 3
MOBA_KEY_GROUP = 4
MOBA_Q_TILE = 1024
SSD_INNER = 2048
SSD_HEADS = 32
SSD_GROUPS = 8
SSD_STATE = 128
SSD_CONV = 4
SSD_CHUNK = 256
SSD_GN = SSD_GROUPS * SSD_STATE
SSD_GROUP_W = SSD_INNER // SSD_GROUPS
SSD_GROUPS_PER_STEP = 2
PEER_HEADS = 8
PEER_NKEYS = 128
PEER_QDIM = 256
PEER_TOPK = 16
PEER_SLOTS = PEER_HEADS * PEER_TOPK
PEER_CHUNKS = 4
PEER_TOKENS_PER_STEP = 32

LANES = 128
SUBLANES = 8
VMEM_LIMIT = 56 * 1024 * 1024
MASK_BIG = 1e30

COL_Q, COL_K, COL_V = 0, ATTN_WIDTH, 2 * ATTN_WIDTH
COL_Z = 3 * ATTN_WIDTH
COL_X = COL_Z + SSD_INNER
COL_B = COL_X + SSD_INNER
COL_C = COL_B + SSD_GN
COL_GA = COL_C + SSD_GN
COL_GS = COL_GA + D_MODEL
PROJ_COLS = COL_GS + D_MODEL

NT_DIMS = (((1,), (1,)), ((), ()))


def _cparams(sem):
    return pltpu.CompilerParams(dimension_semantics=sem, vmem_limit_bytes=VMEM_LIMIT)


def _norm_matmul_kernel(x_ref, g_ref, w_ref, o_ref, h_ref):
    @pl.when(pl.program_id(1) == 0)
    def _():
        x = x_ref[...]
        y = x * lax.rsqrt(jnp.mean(x * x, axis=-1, keepdims=True) + EPS)
        h_ref[...] = (y * g_ref[...]).astype(h_ref.dtype)

    o_ref[...] = jnp.dot(h_ref[...], w_ref[...], preferred_element_type=F32).astype(o_ref.dtype)


def norm_matmul(x, g, w, out_dtype, tm, tn):
    T, K = x.shape
    N = w.shape[1]
    return pl.pallas_call(
        _norm_matmul_kernel,
        grid=(T // tm, N // tn),
        in_specs=[
            pl.BlockSpec((tm, K), lambda i, j: (i, 0)),
            pl.BlockSpec((1, K), lambda i, j: (0, 0)),
            pl.BlockSpec((K, tn), lambda i, j: (0, j)),
        ],
        out_specs=pl.BlockSpec((tm, tn), lambda i, j: (i, j)),
        out_shape=jax.ShapeDtypeStruct((T, N), out_dtype),
        scratch_shapes=[pltpu.VMEM((tm, K), BF16)],
        compiler_params=_cparams(("parallel", "arbitrary")),
        name="norm_matmul",
    )(x, g, w)


def _split3(v):
    hi = v.astype(BF16).astype(F32)
    r1 = v - hi
    mid = r1.astype(BF16).astype(F32)
    return hi, mid, r1 - mid


def _head_pair_norm(x, g):
    lane = lax.broadcasted_iota(I32, x.shape, 1)
    low = lane < HEAD_DIM
    x2 = x * x
    ss_a = jnp.sum(jnp.where(low, x2, 0.0), axis=-1, keepdims=True)
    ss_b = jnp.sum(jnp.where(low, 0.0, x2), axis=-1, keepdims=True)
    inv = jnp.where(low, lax.rsqrt(ss_a / HEAD_DIM + EPS), lax.rsqrt(ss_b / HEAD_DIM + EPS))
    return x * inv * g


def _kprep_kernel(k_ref, v_ref, g_ref, sl_ref, kaug_ref, kmean_ref, vt_ref, *, tq, tk):
    s_idx = pl.program_id(2)
    for grp in range(tq // tk):
        vblk = v_ref[grp * tk:(grp + 1) * tk, :].astype(F32)
        vt_ref[grp] = vblk.T.astype(BF16)
    kn = _head_pair_norm(k_ref[...].astype(F32), g_ref[...])
    nb = tq // MOBA_BLOCK
    km = jnp.mean(kn.reshape(nb, MOBA_BLOCK, LANES), axis=1)
    lane = lax.broadcasted_iota(I32, (tq, LANES), 1)
    row = lax.broadcasted_iota(I32, (tq, LANES), 0) + s_idx * tq
    blk = row // MOBA_BLOCK
    pos = row.astype(F32)
    lane_m = lax.broadcasted_iota(I32, (nb, LANES), 1)
    heads = ((kn, km), (pltpu.roll(kn, HEAD_DIM, axis=1), pltpu.roll(km, HEAD_DIM, axis=1)))
    for hh, (kk, kmm) in enumerate(heads):
        hi, mid, lo = _split3(sl_ref[hh:hh + 1, :] * pos)
        aug = jnp.where(lane < HEAD_DIM, kk, 0.0)
        aug = jnp.where((lane >= 64) & (lane < 96), (lane - 64 == blk).astype(F32), aug)
        aug = jnp.where(lane == 96, hi, aug)
        aug = jnp.where(lane == 97, mid, aug)
        aug = jnp.where(lane == 98, lo, aug)
        aug = jnp.where((lane >= 99) & (lane < 102), 1.0, aug)
        kaug_ref[hh] = aug.astype(BF16)
        kmean_ref[hh] = jnp.where(lane_m < HEAD_DIM, kmm, 0.0)


def _qprep_kernel(q_ref, g_ref, sl_ref, kmean_ref, qaug_ref, *, tq, nblk):
    s_idx = pl.program_id(2)
    qn = _head_pair_norm(q_ref[...].astype(F32), g_ref[...])
    lane = lax.broadcasted_iota(I32, (tq, LANES), 1)
    t = (lax.broadcasted_iota(I32, (tq, LANES), 0) + s_idx * tq).astype(F32)
    jrow = lax.broadcasted_iota(I32, (32, tq), 0)
    own = (lax.broadcasted_iota(I32, (32, tq), 1) + s_idx * tq) // MOBA_BLOCK
    heads = (qn, pltpu.roll(qn, HEAD_DIM, axis=1))
    for hh, qh in enumerate(heads):
        qq = jnp.where(lane < HEAD_DIM, qh, 0.0)
        km_rows = jnp.concatenate([kmean_ref[hh], jnp.zeros((32 - nblk, LANES), F32)], axis=0) \
            if nblk < 32 else kmean_ref[hh]
        g = lax.dot_general(km_rows, qq, NT_DIMS, precision=lax.Precision.HIGHEST,
                            preferred_element_type=F32)
        g = jnp.where(jrow < own, g, -jnp.inf)
        allowed = jrow == own
        for r in range(MOBA_TOPK):
            m = jnp.max(g, axis=0, keepdims=True)
            first = jnp.min(jnp.where(g == m, jrow, 1 << 20), axis=0, keepdims=True)
            hit = jrow == first
            allowed = allowed | (hit & (own > r))
            g = jnp.where(hit, -jnp.inf, g)
        mask_t = jnp.where(allowed, 0.0, -MASK_BIG)
        mask = jnp.concatenate([jnp.zeros((64, tq), F32), mask_t, jnp.zeros((32, tq), F32)], axis=0).T
        hi, mid, lo = _split3(-sl_ref[hh:hh + 1, :] * t)
        aug = qq * (HEAD_DIM ** -0.5)
        aug = jnp.where((lane >= 64) & (lane < 96), mask, aug)
        aug = jnp.where((lane >= 96) & (lane < 99), 1.0, aug)
        aug = jnp.where(lane == 99, hi, aug)
        aug = jnp.where(lane == 100, mid, aug)
        aug = jnp.where(lane == 101, lo, aug)
        qaug_ref[hh] = aug.astype(BF16)


def _attn_kernel(q_ref, k_ref, vt_ref, o_ref, acc_ref, m_ref, *, tq, tk, q_off):
    i = pl.program_id(2) + q_off
    vrow = lax.broadcasted_iota(I32, (LANES, tk), 0)
    low = vrow < HEAD_DIM
    last = (i * tq) // tk

    def group(g, diag):
        start = pl.multiple_of(g * tk, tk)
        vt = vt_ref[g]
        one = jnp.ones_like(vt)
        if diag:
            kpos = lax.broadcasted_iota(I32, (tk, tq), 0) + g * tk
            qpos = lax.broadcasted_iota(I32, (tk, tq), 1) + i * tq
            causal = kpos <= qpos
        for hh in range(2):
            kj = k_ref[hh, pl.ds(start, tk), :]
            s = lax.dot_general(kj, q_ref[hh], NT_DIMS, preferred_element_type=F32)
            if diag:
                s = jnp.where(causal, s, -MASK_BIG)
            m_old = m_ref[hh]
            m_new = jnp.maximum(m_old, jnp.max(s, axis=0, keepdims=True))
            alpha = jnp.exp(m_old - m_new)
            p = jnp.exp(s - m_new).astype(BF16)
            vaug = jnp.where(low, vt, one) if hh == 0 else jnp.where(low, one, vt)
            acc_ref[hh] = alpha * acc_ref[hh] + jnp.dot(vaug, p, preferred_element_type=F32)
            m_ref[hh] = m_new

    acc_ref[...] = jnp.zeros_like(acc_ref)
    m_ref[...] = jnp.full_like(m_ref, -MASK_BIG)
    group(last, True)

    def body(g, carry):
        group(g, False)
        return carry

    lax.fori_loop(0, last, body, 0)
    a = acc_ref[0]
    b = acc_ref[1]
    low_q = lax.broadcasted_iota(I32, (LANES, tq), 0) < HEAD_DIM
    out = jnp.where(low_q, a / a[HEAD_DIM:HEAD_DIM + 1, :], b / b[0:1, :])
    o_ref[...] = out.T.astype(o_ref.dtype)


def moba_prepare(proj3, q_norm_g, k_norm_g):
    B, S, _ = proj3.shape
    nblk = S // MOBA_BLOCK
    assert S % MOBA_BLOCK == 0 and nblk <= 32
    HP = N_HEADS // 2
    tq = min(2048, S)
    slopes = jnp.exp2(-8.0 * jnp.arange(1, N_HEADS + 1, dtype=F32) / N_HEADS)
    sl = jnp.zeros((HP, SUBLANES, LANES), F32)
    sl = sl.at[:, 0, :].set(slopes[0::2, None]).at[:, 1, :].set(slopes[1::2, None])
    gq = jnp.tile(q_norm_g.astype(F32), 2)[None, :]
    gk = jnp.tile(k_norm_g.astype(F32), 2)[None, :]
    qb, kb, vb = COL_Q // LANES, COL_K // LANES, COL_V // LANES
    grid = (B, HP, S // tq)
    sem3 = ("parallel", "parallel", "parallel")

    nbt = tq // MOBA_BLOCK
    tk = MOBA_KEY_GROUP * MOBA_BLOCK
    assert tq % tk == 0
    kaug, kmean, vt = pl.pallas_call(
        functools.partial(_kprep_kernel, tq=tq, tk=tk),
        grid=grid,
        in_specs=[
            pl.BlockSpec((None, tq, LANES), lambda b, p, s: (b, s, kb + p)),
            pl.BlockSpec((None, tq, LANES), lambda b, p, s: (b, s, vb + p)),
            pl.BlockSpec((1, LANES), lambda b, p, s: (0, 0)),
            pl.BlockSpec((None, SUBLANES, LANES), lambda b, p, s: (p, 0, 0)),
        ],
        out_specs=[
            pl.BlockSpec((None, 2, tq, LANES), lambda b, p, s: (b, p, s, 0)),
            pl.BlockSpec((None, 2, nbt, LANES), lambda b, p, s: (b, p, s, 0)),
            pl.BlockSpec((None, None, tq // tk, LANES, tk), lambda b, p, s: (b, p, s, 0, 0)),
        ],
        out_shape=[
            jax.ShapeDtypeStruct((B, N_HEADS, S, LANES), BF16),
            jax.ShapeDtypeStruct((B, N_HEADS, nblk, LANES), F32),
            jax.ShapeDtypeStruct((B, HP, S // tk, LANES, tk), BF16),
        ],
        compiler_params=_cparams(sem3),
        name="moba_kprep",
    )(proj3, proj3, gk, sl)

    qaug = pl.pallas_call(
        functools.partial(_qprep_kernel, tq=tq, nblk=nblk),
        grid=grid,
        in_specs=[
            pl.BlockSpec((None, tq, LANES), lambda b, p, s: (b, s, qb + p)),
            pl.BlockSpec((1, LANES), lambda b, p, s: (0, 0)),
            pl.BlockSpec((None, SUBLANES, LANES), lambda b, p, s: (p, 0, 0)),
            pl.BlockSpec((None, 2, nblk, LANES), lambda b, p, s: (b, p, 0, 0)),
        ],
        out_specs=pl.BlockSpec((None, 2, tq, LANES), lambda b, p, s: (b, p, s, 0)),
        out_shape=jax.ShapeDtypeStruct((B, N_HEADS, S, LANES), BF16),
        compiler_params=_cparams(sem3),
        name="moba_qprep",
    )(proj3, gq, sl, kmean)

    return qaug, kaug, vt


def moba_attention(qaug, kaug, vt, row_lo, row_hi):
    B, _, S, _ = qaug.shape
    HP = N_HEADS // 2
    tk = MOBA_KEY_GROUP * MOBA_BLOCK
    ta = min(MOBA_Q_TILE, row_hi - row_lo)
    assert tk % ta == 0 and row_lo % ta == 0 and (row_hi - row_lo) % ta == 0
    q_off = row_lo // ta
    return pl.pallas_call(
        functools.partial(_attn_kernel, tq=ta, tk=tk, q_off=q_off),
        grid=(B, HP, (row_hi - row_lo) // ta),
        in_specs=[
            pl.BlockSpec((None, 2, ta, LANES), lambda b, p, i: (b, p, i + q_off, 0)),
            pl.BlockSpec((None, 2, S, LANES), lambda b, p, i: (b, p, 0, 0)),
            pl.BlockSpec((None, None, S // tk, LANES, tk), lambda b, p, i: (b, p, 0, 0, 0)),
        ],
        out_specs=pl.BlockSpec((None, ta, LANES), lambda b, p, i: (b, i, p)),
        out_shape=jax.ShapeDtypeStruct((B, row_hi - row_lo, ATTN_WIDTH), BF16),
        scratch_shapes=[pltpu.VMEM((2, LANES, ta), F32), pltpu.VMEM((2, 1, ta), F32)],
        compiler_params=_cparams(("parallel", "parallel", "arbitrary")),
        name="moba_attn",
    )(qaug, kaug, vt)


def _silu(x):
    return x * (1.0 / (1.0 + jnp.exp(-x)))


def _conv_silu(u_ref, halo_ref, w_ref, b_ref, first):
    u = u_ref[...].astype(F32)
    halo = jnp.where(first, 0.0, halo_ref[...].astype(F32))
    ext = jnp.concatenate([halo, u], axis=0)
    w = w_ref[...]
    out = b_ref[...] + w[3:4, :] * u
    for back in range(1, SSD_CONV):
        out = out + w[3 - back:4 - back, :] * pltpu.roll(ext, back, axis=0)[SUBLANES:, :]
    return _silu(out)


def _ssd_group(xs, bm, cm, z, dt, rg, a, d, ng, state_ref):
    L, W = xs.shape
    hp = lax.Precision.HIGHEST
    dtx = jnp.dot(dt, rg, precision=hp, preferred_element_type=F32)
    ax = dtx * a
    rr = lax.broadcasted_iota(I32, (L, L), 0)
    cc = lax.broadcasted_iota(I32, (L, L), 1)
    causal = cc <= rr
    acum = jnp.dot(causal.astype(F32), ax, precision=hp, preferred_element_type=F32)
    acum_t = acum.T
    a_last = acum[L - 1:L, :]
    xdt = xs * dtx
    cmb = cm.astype(BF16)
    cb = lax.dot_general(cmb, bm.astype(BF16), NT_DIMS, preferred_element_type=F32)
    lane = lax.broadcasted_iota(I32, (L, W), 1)
    y = jnp.zeros((L, W), F32)
    for r in range(W // 64):
        seg = acum[:, 64 * r:64 * r + 1] - acum_t[64 * r:64 * r + 1, :]
        lmat = jnp.exp(jnp.where(causal, seg, -jnp.inf))
        xr = jnp.where((lane >= 64 * r) & (lane < 64 * r + 64), xdt, 0.0).astype(BF16)
        y = y + jnp.dot((cb * lmat).astype(BF16), xr, preferred_element_type=F32)
    state = state_ref[...]
    y = y + jnp.dot(cmb, state.astype(BF16), preferred_element_type=F32) * jnp.exp(acum)
    wgt = (xdt * jnp.exp(a_last - acum)).astype(BF16)
    state_ref[...] = state * jnp.exp(a_last) + jnp.dot(bm.T.astype(BF16), wgt, preferred_element_type=F32)
    y = y + d * xs
    y = y * _silu(z)
    y = y * lax.rsqrt(jnp.mean(y * y, axis=-1, keepdims=True) + EPS)
    return y * ng


def _ssd_kernel(x_ref, xh_ref, b_ref, bh_ref, c_ref, ch_ref, z_ref, dt_ref,
                wx_ref, wb_ref, wc_ref, bx_ref, bb_ref, bc_ref, dtb_ref, rg_ref,
                a_ref, d_ref, ng_ref, y_ref, state_ref):
    first = pl.program_id(2) == 0
    W, N = SSD_GROUP_W, SSD_STATE

    @pl.when(first)
    def _():
        state_ref[...] = jnp.zeros_like(state_ref)

    xs = _conv_silu(x_ref, xh_ref, wx_ref, bx_ref, first)
    bm = _conv_silu(b_ref, bh_ref, wb_ref, bb_ref, first)
    cm = _conv_silu(c_ref, ch_ref, wc_ref, bc_ref, first)
    dt = jax.nn.softplus(dt_ref[...] + dtb_ref[...])
    z = z_ref[...].astype(F32)
    for gi in range(SSD_GROUPS_PER_STEP):
        cw, cn = slice(gi * W, (gi + 1) * W), slice(gi * N, (gi + 1) * N)
        y = _ssd_group(xs[:, cw], bm[:, cn], cm[:, cn], z[:, cw], dt, rg_ref[gi],
                       a_ref[:, cw], d_ref[:, cw], ng_ref[:, cw], state_ref.at[gi])
        y_ref[:, cw] = y.astype(y_ref.dtype)


def ssd_mixer(proj3, dtraw3, conv_w, conv_b, dt_bias, a_log, d_skip, norm_g):
    B, S, _ = proj3.shape
    GP = SSD_GROUPS_PER_STEP
    L, W, N = SSD_CHUNK, SSD_GROUP_W * GP, SSD_STATE * GP
    assert S % L == 0 and SSD_GROUPS % GP == 0
    nc = S // L
    hb = L // SUBLANES
    xb, bb, cb_, zb = COL_X // W, COL_B // N, COL_C // N, COL_Z // W
    rep = SSD_INNER // SSD_HEADS
    a_exp = jnp.repeat(-jnp.exp(a_log.astype(F32)), rep)[None, :]
    d_exp = jnp.repeat(d_skip.astype(F32), rep)[None, :]
    ng = norm_g.astype(F32)[None, :]
    dtb = jnp.zeros((1, LANES), F32).at[0, :SSD_HEADS].set(dt_bias.astype(F32))
    head_of_chan = np.arange(SSD_INNER) // rep
    rg = (np.arange(LANES)[None, :, None]
          == head_of_chan.reshape(SSD_GROUPS, 1, SSD_GROUP_W)).astype(np.float32)
    cw = conv_w.astype(F32)
    cbias = conv_b.astype(F32)[None, :]
    cxo, cbo, cco = 0, SSD_INNER // N, (SSD_INNER + SSD_GN) // N

    def halo(col):
        return lambda b, g, c: (b, jnp.maximum(c * hb - 1, 0), col + g)

    return pl.pallas_call(
        _ssd_kernel,
        grid=(B, SSD_GROUPS // GP, nc),
        in_specs=[
            pl.BlockSpec((None, L, W), lambda b, g, c: (b, c, xb + g)),
            pl.BlockSpec((None, SUBLANES, W), halo(xb)),
            pl.BlockSpec((None, L, N), lambda b, g, c: (b, c, bb + g)),
            pl.BlockSpec((None, SUBLANES, N), halo(bb)),
            pl.BlockSpec((None, L, N), lambda b, g, c: (b, c, cb_ + g)),
            pl.BlockSpec((None, SUBLANES, N), halo(cb_)),
            pl.BlockSpec((None, L, W), lambda b, g, c: (b, c, zb + g)),
            pl.BlockSpec((None, L, LANES), lambda b, g, c: (b, c, 0)),
            pl.BlockSpec((SSD_CONV, W), lambda b, g, c: (0, cxo + g)),
            pl.BlockSpec((SSD_CONV, N), lambda b, g, c: (0, cbo + g)),
            pl.BlockSpec((SSD_CONV, N), lambda b, g, c: (0, cco + g)),
            pl.BlockSpec((1, W), lambda b, g, c: (0, cxo + g)),
            pl.BlockSpec((1, N), lambda b, g, c: (0, cbo + g)),
            pl.BlockSpec((1, N), lambda b, g, c: (0, cco + g)),
            pl.BlockSpec((1, LANES), lambda b, g, c: (0, 0)),
            pl.BlockSpec((GP, LANES, SSD_GROUP_W), lambda b, g, c: (g, 0, 0)),
            pl.BlockSpec((1, W), lambda b, g, c: (0, g)),
            pl.BlockSpec((1, W), lambda b, g, c: (0, g)),
            pl.BlockSpec((1, W), lambda b, g, c: (0, g)),
        ],
        out_specs=pl.BlockSpec((None, L, W), lambda b, g, c: (b, c, g)),
        out_shape=jax.ShapeDtypeStruct((B, S, SSD_INNER), BF16),
        scratch_shapes=[pltpu.VMEM((GP, SSD_STATE, SSD_GROUP_W), F32)],
        compiler_params=_cparams(("parallel", "parallel", "arbitrary")),
        name="ssd_scan",
    )(proj3, proj3, proj3, proj3, proj3, proj3, proj3, dtraw3,
      cw, cw, cw, cbias, cbias, cbias, dtb, jnp.asarray(rg), a_exp, d_exp, ng)


def _merge_kernel(x_ref, a_ref, s_ref, ga_ref, gs_ref, wa_ref, ws_ref, wo_ref, o_ref):
    ya = jnp.dot(a_ref[...], wa_ref[...], preferred_element_type=F32)
    ys = jnp.dot(s_ref[...], ws_ref[...], preferred_element_type=F32)
    mixed = jax.nn.sigmoid(ga_ref[...].astype(F32)) * ya + jax.nn.sigmoid(gs_ref[...].astype(F32)) * ys
    o_ref[...] = x_ref[...] + jnp.dot(mixed.astype(BF16), wo_ref[...], preferred_element_type=F32)


def merge_branches(x2, attn2, ssd2, proj2, wa, ws, wo, tm, row_lo):
    T, D = attn2.shape[0], x2.shape[1]
    off = row_lo // tm
    assert row_lo % tm == 0
    full = lambda a: pl.BlockSpec(a.shape, lambda i: (0, 0))
    return pl.pallas_call(
        _merge_kernel,
        grid=(T // tm,),
        in_specs=[
            pl.BlockSpec((tm, D), lambda i: (i + off, 0)),
            pl.BlockSpec((tm, ATTN_WIDTH), lambda i: (i, 0)),
            pl.BlockSpec((tm, SSD_INNER), lambda i: (i + off, 0)),
            pl.BlockSpec((tm, D), lambda i: (i + off, COL_GA // D)),
            pl.BlockSpec((tm, D), lambda i: (i + off, COL_GS // D)),
            full(wa), full(ws), full(wo),
        ],
        out_specs=pl.BlockSpec((tm, D), lambda i: (i, 0)),
        out_shape=jax.ShapeDtypeStruct((T, D), F32),
        compiler_params=_cparams(("parallel",)),
        name="merge",
    )(x2, attn2, ssd2, proj2, proj2, wa, ws, wo)


def _topk_rows(s, idx, k):
    n = s.shape[0]
    row = lax.broadcasted_iota(I32, s.shape, 0)
    vals, rows, picked = [], [], []
    for _ in range(k):
        m = jnp.max(s, axis=0, keepdims=True)
        first = jnp.min(jnp.where(s == m, row, n), axis=0, keepdims=True)
        hit = row == first
        vals.append(m)
        rows.append(first)
        if idx is not None:
            picked.append(jnp.max(jnp.where(hit, idx, -1), axis=0, keepdims=True))
        s = jnp.where(hit, -jnp.inf, s)
    cat = lambda xs: jnp.concatenate(xs, axis=0)
    return cat(vals), cat(rows), (cat(picked) if idx is not None else None)


def _peer_topk_kernel(q_ref, k1_ref, k2_ref, eidx_ref, gw_ref, *, tt):
    half = PEER_QDIM // 2
    e_rows, g_rows = [], []
    for h in range(PEER_HEADS):
        qa = q_ref[:, h * PEER_QDIM:h * PEER_QDIM + half].astype(BF16)
        qb = q_ref[:, h * PEER_QDIM + half:(h + 1) * PEER_QDIM].astype(BF16)
        s1 = lax.dot_general(k1_ref[h], qa, NT_DIMS, preferred_element_type=F32)
        s2 = lax.dot_general(k2_ref[h], qb, NT_DIMS, preferred_element_type=F32)
        v1, i1, _ = _topk_rows(s1, None, PEER_TOPK)
        v2, i2, _ = _topk_rows(s2, None, PEER_TOPK)
        sub = lax.broadcasted_iota(I32, (SUBLANES, tt), 0)
        cand_parts = [v1[0:1, :] + v2]
        cidx_parts = [i1[0:1, :] * PEER_NKEYS + i2]
        for a in range(1, SUBLANES):
            keep = sub < PEER_TOPK // (a + 1)
            cand_parts.append(jnp.where(keep, v1[a:a + 1, :] + v2[0:SUBLANES, :], -jnp.inf))
            cidx_parts.append(i1[a:a + 1, :] * PEER_NKEYS + i2[0:SUBLANES, :])
        cand_parts.append(v1[SUBLANES:, :] + v2[0:1, :])
        cidx_parts.append(i1[SUBLANES:, :] * PEER_NKEYS + i2[0:1, :])
        cand = jnp.concatenate(cand_parts, axis=0)
        cidx = jnp.concatenate(cidx_parts, axis=0)
        sv, _, ex = _topk_rows(cand, cidx, PEER_TOPK)
        e = jnp.exp(sv - sv[0:1, :])
        g_rows.append(e / jnp.sum(e, axis=0, keepdims=True))
        e_rows.append(ex)
    eidx_ref[...] = jnp.concatenate(e_rows, axis=0).T
    gw_ref[...] = jnp.concatenate(g_rows, axis=0).T


def peer_topk(q, keys1, keys2, tt):
    T = q.shape[0]
    full3 = lambda a: pl.BlockSpec(a.shape, lambda i: (0, 0, 0))
    return pl.pallas_call(
        functools.partial(_peer_topk_kernel, tt=tt),
        grid=(T // tt,),
        in_specs=[pl.BlockSpec((tt, q.shape[1]), lambda i: (i, 0)), full3(keys1), full3(keys2)],
        out_specs=[
            pl.BlockSpec((tt, PEER_SLOTS), lambda i: (i, 0)),
            pl.BlockSpec((tt, PEER_SLOTS), lambda i: (i, 0)),
        ],
        out_shape=[
            jax.ShapeDtypeStruct((T, PEER_SLOTS), I32),
            jax.ShapeDtypeStruct((T, PEER_SLOTS), F32),
        ],
        compiler_params=_cparams(("parallel",)),
        name="peer_topk",
    )(q, keys1, keys2)


def pack_table(tab):
    half = tab.shape[1] // 2
    bits = lax.bitcast_convert_type(tab.astype(BF16), jnp.uint16).astype(jnp.uint32)
    return lax.bitcast_convert_type((bits[:, :half] << 16) | bits[:, half:], I32)


def sc_gather_rows(table, idx, win=64):
    V, W = table.shape
    N = idx.shape[0]
    info = plsc.get_sparse_core_info()
    n_cores, n_sub = info.num_cores, info.num_subcores
    workers = n_cores * n_sub
    per_w = N // workers
    steps = per_w // win
    assert steps * win * workers == N and steps % 2 == 0
    mesh = plsc.VectorSubcoreMesh(core_axis_name="c", subcore_axis_name="s")
    dma = pltpu.SemaphoreType.DMA

    @functools.partial(
        pl.kernel, mesh=mesh,
        out_type=jax.ShapeDtypeStruct((N, W), table.dtype),
        scratch_types=[
            pltpu.VMEM((steps, win), I32),
            pltpu.VMEM((win, W), table.dtype),
            pltpu.VMEM((win, W), table.dtype),
            dma, dma, dma, dma,
        ],
    )
    def gather_kernel(table_hbm, idx_hbm, out_hbm, idx_v, rows0, rows1, g0, g1, w0, w1):
        wid = lax.axis_index("s") * n_cores + lax.axis_index("c")
        row0 = wid * steps
        pltpu.sync_copy(idx_hbm.at[pl.ds(row0, steps)], idx_v)
        slots = ((rows0, g0, w0), (rows1, g1, w1))

        def gather(i, slot):
            rows, gsem, _ = slots[slot]
            return pltpu.make_async_copy(table_hbm.at[idx_v.at[i]], rows, gsem)

        def write(i, slot):
            rows, _, wsem = slots[slot]
            off = pl.multiple_of((row0 + i) * win, win)
            return pltpu.make_async_copy(rows, out_hbm.at[pl.ds(off, win)], wsem)

        gather(0, 0).start()

        @pl.loop(0, steps, step=2)
        def _(i):
            @pl.when(i > 0)
            def _():
                write(i - 1, 1).wait()

            gather(i + 1, 1).start()
            gather(i, 0).wait()
            write(i, 0).start()
            write(i, 0).wait()

            @pl.when(i + 2 < steps)
            def _():
                gather(i + 2, 0).start()

            gather(i + 1, 1).wait()
            write(i + 1, 1).start()

        write(steps - 1, 1).wait()

    return gather_kernel(table, idx.reshape(N // win, win))


def sc_expert_scores(table, idx, xn, win=32):
    V, D = table.shape
    N = idx.shape[0]
    T = xn.shape[0]
    slots = N // T
    info = plsc.get_sparse_core_info()
    n_cores, n_sub, L = info.num_cores, info.num_subcores, info.num_lanes
    workers = n_cores * n_sub
    tok_w = T // workers
    per_tok = slots // win
    total = tok_w * per_tok
    assert tok_w * workers == T and per_tok * win == slots and per_tok % 2 == 0
    assert xn.shape[1] == D and D % L == 0 and win % L == 0
    mesh = plsc.VectorSubcoreMesh(core_axis_name="c", subcore_axis_name="s")
    dma = pltpu.SemaphoreType.DMA

    @functools.partial(
        pl.kernel, mesh=mesh,
        out_type=jax.ShapeDtypeStruct((N,), F32),
        scratch_types=[pltpu.VMEM((total, win), I32),
                       pltpu.VMEM((win, D), F32), pltpu.VMEM((win, D), F32),
                       pltpu.VMEM((D,), F32), pltpu.VMEM((slots,), F32), dma, dma],
        compiler_params=pltpu.CompilerParams(needs_layout_passes=False),
    )
    def k(table_hbm, idx_hbm, xn_hbm, act_hbm, idx_v, rows0, rows1, x_v, act_v, g0, g1):
        wid = lax.axis_index("s") * n_cores + lax.axis_index("c")
        tok0 = wid * tok_w
        pltpu.sync_copy(idx_hbm.at[pl.ds(tok0 * per_tok, total)], idx_v)
        bufs = ((rows0, g0), (rows1, g1))

        def gather(step, slot):
            rows, sem = bufs[slot]
            return pltpu.make_async_copy(table_hbm.at[idx_v.at[step]], rows, sem)

        lane = lax.iota(I32, L)

        def compute(rows, base):
            def body(kk, accs):
                k16 = pl.multiple_of(kk * L, L)
                xc = x_v[pl.ds(k16, L)]
                return tuple(accs[r] + rows[r, pl.ds(k16, L)] * xc for r in range(win))

            accs = lax.fori_loop(0, D // L, body, tuple(jnp.zeros((L,), F32) for _ in range(win)))
            for j in range(win // L):
                res = jnp.zeros((L,), F32)
                for r in range(L):
                    res = jnp.where(lane == r, jnp.sum(accs[j * L + r]), res)
                act_v[pl.ds(base + j * L, L)] = res

        gather(0, 0).start()

        @pl.loop(0, tok_w)
        def _(t):
            pltpu.sync_copy(xn_hbm.at[tok0 + t], x_v)
            for h in range(per_tok):
                step = t * per_tok + h

                @pl.when(step + 1 < total)
                def _():
                    gather(step + 1, (h + 1) % 2).start()

                gather(step, h % 2).wait()
                compute(bufs[h % 2][0], h * win)
            pltpu.sync_copy(act_v, act_hbm.at[pl.ds(pl.multiple_of((tok0 + t) * slots, slots), slots)])

    return k(table, idx.reshape(N // win, win), xn)


def _unpack_words(w):
    u = pltpu.bitcast(w, jnp.uint32)
    hi = pltpu.bitcast(u & jnp.uint32(0xFFFF0000), F32)
    lo = pltpu.bitcast(u << 16, F32)
    return hi, lo


def _rms_scale_kernel(x_ref, g_ref, o_ref):
    x = x_ref[...]
    o_ref[...] = x * lax.rsqrt(jnp.mean(x * x, axis=-1, keepdims=True) + EPS) * g_ref[...]


def rms_scale(x, g, tm):
    T, D = x.shape
    return pl.pallas_call(
        _rms_scale_kernel,
        grid=(T // tm,),
        in_specs=[pl.BlockSpec((tm, D), lambda i: (i, 0)), pl.BlockSpec((1, D), lambda i: (0, 0))],
        out_specs=pl.BlockSpec((tm, D), lambda i: (i, 0)),
        out_shape=jax.ShapeDtypeStruct((T, D), F32),
        compiler_params=_cparams(("parallel",)),
        name="rms_scale",
    )(x, g)


def _peer_expert_kernel(x_ref, act_ref, gw_ref, vg_ref, o_ref, *, tt):
    half = D_MODEL // 2
    x1 = x_ref[...]
    act = act_ref[...]
    gelu = 0.5 * act * (1.0 + lax.erf(act * (2.0 ** -0.5)))
    hact = gelu * gw_ref[...]
    hact_t = jnp.concatenate([hact, jnp.zeros((LANES - tt, PEER_SLOTS), F32)], axis=0).T
    lane = lax.broadcasted_iota(I32, (PEER_SLOTS, LANES), 1)
    for t in range(tt):
        hcol = jnp.sum(jnp.where(lane == t, hact_t, 0.0), axis=-1, keepdims=True)
        hi, lo = _unpack_words(vg_ref[t * PEER_SLOTS:(t + 1) * PEER_SLOTS, :])
        o_ref[t:t + 1, :half] = x1[t:t + 1, :half] + jnp.sum(hcol * hi, axis=0, keepdims=True)
        o_ref[t:t + 1, half:] = x1[t:t + 1, half:] + jnp.sum(hcol * lo, axis=0, keepdims=True)


def peer_experts(x1, act, gw, vg, tt=PEER_TOKENS_PER_STEP):
    T, D = x1.shape
    W = vg.shape[1]
    return pl.pallas_call(
        functools.partial(_peer_expert_kernel, tt=tt),
        grid=(T // tt,),
        in_specs=[
            pl.BlockSpec((tt, D), lambda i: (i, 0)),
            pl.BlockSpec((tt, PEER_SLOTS), lambda i: (i, 0)),
            pl.BlockSpec((tt, PEER_SLOTS), lambda i: (i, 0)),
            pl.BlockSpec((tt * PEER_SLOTS, W), lambda i: (i, 0)),
        ],
        out_specs=pl.BlockSpec((tt, D), lambda i: (i, 0)),
        out_shape=jax.ShapeDtypeStruct((T, D), F32),
        compiler_params=_cparams(("parallel",)),
        name="peer_experts",
    )(x1, act, gw, vg)


def peer_ffn_residual(x1, norm2_g, wq, keys1, keys2, u_tab, v_packed):
    T = x1.shape[0]
    g2 = norm2_g.astype(F32)[None, :]
    q = norm_matmul(x1, g2, wq, F32, min(1024, T), 1024)
    eidx, gw = peer_topk(q, keys1, keys2, min(256, T))
    flat = eidx.reshape(-1)
    xn = rms_scale(x1, g2, min(512, T))
    act = sc_expert_scores(u_tab, flat, xn).reshape(T, PEER_SLOTS)
    vg = sc_gather_rows(v_packed, flat)
    return peer_experts(x1, act, gw, vg)


def kernel(x, norm1_g, w_in, q_norm_g, k_norm_g, conv_w, conv_b, dt_bias, a_log, d_skip, ssd_norm_g,
           w_attn_o, w_ssd_o, w_out, norm2_g, w_peer_q, peer_keys1, peer_keys2, peer_u, peer_v):
    B, S, D = x.shape
    xs = [x[b] for b in range(B)]
    for l in range(norm1_g.shape[0]):
        w = w_in[l]
        dt0 = COL_GA
        w_main = jnp.concatenate([w[:, :dt0], w[:, dt0 + SSD_HEADS:]], axis=1).astype(BF16)
        w_dt = jnp.zeros((D, LANES), BF16).at[:, :SSD_HEADS].set(w[:, dt0:dt0 + SSD_HEADS].astype(BF16))
        g1 = norm1_g[l].astype(F32)[None, :]
        wa, ws, wo = w_attn_o[l].astype(BF16), w_ssd_o[l].astype(BF16), w_out[l].astype(BF16)
        wq = w_peer_q[l].astype(BF16)
        k1, k2 = peer_keys1[l].astype(BF16), peer_keys2[l].astype(BF16)
        up, vp = peer_u[l].astype(F32), pack_table(peer_v[l])
        tm = min(1024, S)
        rows = S // min(PEER_CHUNKS, S // MOBA_Q_TILE) if S >= MOBA_Q_TILE else S
        for b in range(B):
            x2 = xs[b]
            proj = norm_matmul(x2, g1, w_main, BF16, tm, 1024)
            dtraw = norm_matmul(x2, g1, w_dt, F32, tm, LANES)
            proj3 = proj[None]
            yssd = ssd_mixer(proj3, dtraw[None], conv_w[l], conv_b[l], dt_bias[l],
                             a_log[l], d_skip[l], ssd_norm_g[l])
            qaug, kaug, vt = moba_prepare(proj3, q_norm_g[l], k_norm_g[l])
            outs = []
            for lo in range(0, S, rows):
                attn = moba_attention(qaug, kaug, vt, lo, lo + rows)
                x1 = merge_branches(x2, attn[0], yssd[0], proj, wa, ws, wo, min(512, rows), lo)
                outs.append(peer_ffn_residual(x1, norm2_g[l], wq, k1, k2, up, vp))
            xs[b] = jnp.concatenate(outs, axis=0)
    return jnp.stack(xs, axis=0)
```

```python
import functools

import jax
import jax.numpy as jnp
import numpy as np
from jax import lax
from jax.experimental import pallas as pl
from jax.experimental.pallas import tpu as pltpu
from jax.experimental.pallas import tpu_sc as plsc

F32 = jnp.float32
BF16 = jnp.bfloat16
I32 = jnp.int32

EPS = 1e-6
D_MODEL = 1024
N_HEADS = 16
HEAD_DIM = 64
ATTN_WIDTH = N_HEADS * HEAD_DIM
MOBA_BLOCK = 256
MOBA_TOPK = 3
MOBA_KEY_GROUP = 4
MOBA_Q_TILE = 1024
SSD_INNER = 2048
SSD_HEADS = 32
SSD_GROUPS = 8
SSD_STATE = 128
SSD_CONV = 4
SSD_CHUNK = 256
SSD_GN = SSD_GROUPS * SSD_STATE
SSD_GROUP_W = SSD_INNER // SSD_GROUPS
SSD_GROUPS_PER_STEP = 2
PEER_HEADS = 8
PEER_NKEYS = 128
PEER_QDIM = 256
PEER_TOPK = 16
PEER_SLOTS = PEER_HEADS * PEER_TOPK
PEER_CHUNKS = 4
PEER_TOKENS_PER_STEP = 32

LANES = 128
SUBLANES = 8
VMEM_LIMIT = 56 * 1024 * 1024
MASK_BIG = 1e30

COL_Q, COL_K, COL_V = 0, ATTN_WIDTH, 2 * ATTN_WIDTH
COL_Z = 3 * ATTN_WIDTH
COL_X = COL_Z + SSD_INNER
COL_B = COL_X + SSD_INNER
COL_C = COL_B + SSD_GN
COL_GA = COL_C + SSD_GN
COL_GS = COL_GA + D_MODEL
PROJ_COLS = COL_GS + D_MODEL

NT_DIMS = (((1,), (1,)), ((), ()))


def _cparams(sem):
    return pltpu.CompilerParams(dimension_semantics=sem, vmem_limit_bytes=VMEM_LIMIT)


def _norm_matmul_kernel(x_ref, g_ref, w_ref, o_ref, h_ref):
    @pl.when(pl.program_id(1) == 0)
    def _():
        x = x_ref[...]
        y = x * lax.rsqrt(jnp.mean(x * x, axis=-1, keepdims=True) + EPS)
        h_ref[...] = (y * g_ref[...]).astype(h_ref.dtype)

    o_ref[...] = jnp.dot(h_ref[...], w_ref[...], preferred_element_type=F32).astype(o_ref.dtype)


def norm_matmul(x, g, w, out_dtype, tm, tn):
    T, K = x.shape
    N = w.shape[1]
    return pl.pallas_call(
        _norm_matmul_kernel,
        grid=(T // tm, N // tn),
        in_specs=[
            pl.BlockSpec((tm, K), lambda i, j: (i, 0)),
            pl.BlockSpec((1, K), lambda i, j: (0, 0)),
            pl.BlockSpec((K, tn), lambda i, j: (0, j)),
        ],
        out_specs=pl.BlockSpec((tm, tn), lambda i, j: (i, j)),
        out_shape=jax.ShapeDtypeStruct((T, N), out_dtype),
        scratch_shapes=[pltpu.VMEM((tm, K), BF16)],
        compiler_params=_cparams(("parallel", "arbitrary")),
        name="norm_matmul",
    )(x, g, w)


def _split3(v):
    hi = v.astype(BF16).astype(F32)
    r1 = v - hi
    mid = r1.astype(BF16).astype(F32)
    return hi, mid, r1 - mid


def _head_pair_norm(x, g):
    lane = lax.broadcasted_iota(I32, x.shape, 1)
    low = lane < HEAD_DIM
    x2 = x * x
    ss_a = jnp.sum(jnp.where(low, x2, 0.0), axis=-1, keepdims=True)
    ss_b = jnp.sum(jnp.where(low, 0.0, x2), axis=-1, keepdims=True)
    inv = jnp.where(low, lax.rsqrt(ss_a / HEAD_DIM + EPS), lax.rsqrt(ss_b / HEAD_DIM + EPS))
    return x * inv * g


def _kprep_kernel(k_ref, v_ref, g_ref, sl_ref, kaug_ref, kmean_ref, vt_ref, *, tq, tk):
    s_idx = pl.program_id(2)
    for grp in range(tq // tk):
        vblk = v_ref[grp * tk:(grp + 1) * tk, :].astype(F32)
        vt_ref[grp] = vblk.T.astype(BF16)
    kn = _head_pair_norm(k_ref[...].astype(F32), g_ref[...])
    nb = tq // MOBA_BLOCK
    km = jnp.mean(kn.reshape(nb, MOBA_BLOCK, LANES), axis=1)
    lane = lax.broadcasted_iota(I32, (tq, LANES), 1)
    row = lax.broadcasted_iota(I32, (tq, LANES), 0) + s_idx * tq
    blk = row // MOBA_BLOCK
    pos = row.astype(F32)
    lane_m = lax.broadcasted_iota(I32, (nb, LANES), 1)
    heads = ((kn, km), (pltpu.roll(kn, HEAD_DIM, axis=1), pltpu.roll(km, HEAD_DIM, axis=1)))
    for hh, (kk, kmm) in enumerate(heads):
        hi, mid, lo = _split3(sl_ref[hh:hh + 1, :] * pos)
        aug = jnp.where(lane < HEAD_DIM, kk, 0.0)
        aug = jnp.where((lane >= 64) & (lane < 96), (lane - 64 == blk).astype(F32), aug)
        aug = jnp.where(lane == 96, hi, aug)
        aug = jnp.where(lane == 97, mid, aug)
        aug = jnp.where(lane == 98, lo, aug)
        aug = jnp.where((lane >= 99) & (lane < 102), 1.0, aug)
        kaug_ref[hh] = aug.astype(BF16)
        kmean_ref[hh] = jnp.where(lane_m < HEAD_DIM, kmm, 0.0)


def _qprep_kernel(q_ref, g_ref, sl_ref, kmean_ref, qaug_ref, *, tq, nblk):
    s_idx = pl.program_id(2)
    qn = _head_pair_norm(q_ref[...].astype(F32), g_ref[...])
    lane = lax.broadcasted_iota(I32, (tq, LANES), 1)
    t = (lax.broadcasted_iota(I32, (tq, LANES), 0) + s_idx * tq).astype(F32)
    jrow = lax.broadcasted_iota(I32, (32, tq), 0)
    own = (lax.broadcasted_iota(I32, (32, tq), 1) + s_idx * tq) // MOBA_BLOCK
    heads = (qn, pltpu.roll(qn, HEAD_DIM, axis=1))
    for hh, qh in enumerate(heads):
        qq = jnp.where(lane < HEAD_DIM, qh, 0.0)
        km_rows = jnp.concatenate([kmean_ref[hh], jnp.zeros((32 - nblk, LANES), F32)], axis=0) \
            if nblk < 32 else kmean_ref[hh]
        g = lax.dot_general(km_rows, qq, NT_DIMS, precision=lax.Precision.HIGHEST,
                            preferred_element_type=F32)
        g = jnp.where(jrow < own, g, -jnp.inf)
        allowed = jrow == own
        for r in range(MOBA_TOPK):
            m = jnp.max(g, axis=0, keepdims=True)
            first = jnp.min(jnp.where(g == m, jrow, 1 << 20), axis=0, keepdims=True)
            hit = jrow == first
            allowed = allowed | (hit & (own > r))
            g = jnp.where(hit, -jnp.inf, g)
        mask_t = jnp.where(allowed, 0.0, -MASK_BIG)
        mask = jnp.concatenate([jnp.zeros((64, tq), F32), mask_t, jnp.zeros((32, tq), F32)], axis=0).T
        hi, mid, lo = _split3(-sl_ref[hh:hh + 1, :] * t)
        aug = qq * (HEAD_DIM ** -0.5)
        aug = jnp.where((lane >= 64) & (lane < 96), mask, aug)
        aug = jnp.where((lane >= 96) & (lane < 99), 1.0, aug)
        aug = jnp.where(lane == 99, hi, aug)
        aug = jnp.where(lane == 100, mid, aug)
        aug = jnp.where(lane == 101, lo, aug)
        qaug_ref[hh] = aug.astype(BF16)


def _attn_kernel(q_ref, k_ref, vt_ref, o_ref, acc_ref, m_ref, *, tq, tk, q_off):
    i = pl.program_id(2) + q_off
    vrow = lax.broadcasted_iota(I32, (LANES, tk), 0)
    low = vrow < HEAD_DIM
    last = (i * tq) // tk

    def group(g, diag):
        start = pl.multiple_of(g * tk, tk)
        vt = vt_ref[g]
        one = jnp.ones_like(vt)
        if diag:
            kpos = lax.broadcasted_iota(I32, (tk, tq), 0) + g * tk
            qpos = lax.broadcasted_iota(I32, (tk, tq), 1) + i * tq
            causal = kpos <= qpos
        for hh in range(2):
            kj = k_ref[hh, pl.ds(start, tk), :]
            s = lax.dot_general(kj, q_ref[hh], NT_DIMS, preferred_element_type=F32)
            if diag:
                s = jnp.where(causal, s, -MASK_BIG)
            m_old = m_ref[hh]
            m_new = jnp.maximum(m_old, jnp.max(s, axis=0, keepdims=True))
            alpha = jnp.exp(m_old - m_new)
            p = jnp.exp(s - m_new).astype(BF16)
            vaug = jnp.where(low, vt, one) if hh == 0 else jnp.where(low, one, vt)
            acc_ref[hh] = alpha * acc_ref[hh] + jnp.dot(vaug, p, preferred_element_type=F32)
            m_ref[hh] = m_new

    acc_ref[...] = jnp.zeros_like(acc_ref)
    m_ref[...] = jnp.full_like(m_ref, -MASK_BIG)
    group(last, True)

    def body(g, carry):
        group(g, False)
        return carry

    lax.fori_loop(0, last, body, 0)
    a = acc_ref[0]
    b = acc_ref[1]
    low_q = lax.broadcasted_iota(I32, (LANES, tq), 0) < HEAD_DIM
    out = jnp.where(low_q, a / a[HEAD_DIM:HEAD_DIM + 1, :], b / b[0:1, :])
    o_ref[...] = out.T.astype(o_ref.dtype)


def moba_prepare(proj3, q_norm_g, k_norm_g):
    B, S, _ = proj3.shape
    nblk = S // MOBA_BLOCK
    assert S % MOBA_BLOCK == 0 and nblk <= 32
    HP = N_HEADS // 2
    tq = min(2048, S)
    slopes = jnp.exp2(-8.0 * jnp.arange(1, N_HEADS + 1, dtype=F32) / N_HEADS)
    sl = jnp.zeros((HP, SUBLANES, LANES), F32)
    sl = sl.at[:, 0, :].set(slopes[0::2, None]).at[:, 1, :].set(slopes[1::2, None])
    gq = jnp.tile(q_norm_g.astype(F32), 2)[None, :]
    gk = jnp.tile(k_norm_g.astype(F32), 2)[None, :]
    qb, kb, vb = COL_Q // LANES, COL_K // LANES, COL_V // LANES
    grid = (B, HP, S // tq)
    sem3 = ("parallel", "parallel", "parallel")

    nbt = tq // MOBA_BLOCK
    tk = MOBA_KEY_GROUP * MOBA_BLOCK
    assert tq % tk == 0
    kaug, kmean, vt = pl.pallas_call(
        functools.partial(_kprep_kernel, tq=tq, tk=tk),
        grid=grid,
        in_specs=[
            pl.BlockSpec((None, tq, LANES), lambda b, p, s: (b, s, kb + p)),
            pl.BlockSpec((None, tq, LANES), lambda b, p, s: (b, s, vb + p)),
            pl.BlockSpec((1, LANES), lambda b, p, s: (0, 0)),
            pl.BlockSpec((None, SUBLANES, LANES), lambda b, p, s: (p, 0, 0)),
        ],
        out_specs=[
            pl.BlockSpec((None, 2, tq, LANES), lambda b, p, s: (b, p, s, 0)),
            pl.BlockSpec((None, 2, nbt, LANES), lambda b, p, s: (b, p, s, 0)),
            pl.BlockSpec((None, None, tq // tk, LANES, tk), lambda b, p, s: (b, p, s, 0, 0)),
        ],
        out_shape=[
            jax.ShapeDtypeStruct((B, N_HEADS, S, LANES), BF16),
            jax.ShapeDtypeStruct((B, N_HEADS, nblk, LANES), F32),
            jax.ShapeDtypeStruct((B, HP, S // tk, LANES, tk), BF16),
        ],
        compiler_params=_cparams(sem3),
        name="moba_kprep",
    )(proj3, proj3, gk, sl)

    qaug = pl.pallas_call(
        functools.partial(_qprep_kernel, tq=tq, nblk=nblk),
        grid=grid,
        in_specs=[
            pl.BlockSpec((None, tq, LANES), lambda b, p, s: (b, s, qb + p)),
            pl.BlockSpec((1, LANES), lambda b, p, s: (0, 0)),
            pl.BlockSpec((None, SUBLANES, LANES), lambda b, p, s: (p, 0, 0)),
            pl.BlockSpec((None, 2, nblk, LANES), lambda b, p, s: (b, p, 0, 0)),
        ],
        out_specs=pl.BlockSpec((None, 2, tq, LANES), lambda b, p, s: (b, p, s, 0)),
        out_shape=jax.ShapeDtypeStruct((B, N_HEADS, S, LANES), BF16),
        compiler_params=_cparams(sem3),
        name="moba_qprep",
    )(proj3, gq, sl, kmean)

    return qaug, kaug, vt


def moba_attention(qaug, kaug, vt, row_lo, row_hi):
    B, _, S, _ = qaug.shape
    HP = N_HEADS // 2
    tk = MOBA_KEY_GROUP * MOBA_BLOCK
    ta = min(MOBA_Q_TILE, row_hi - row_lo)
    assert tk % ta == 0 and row_lo % ta == 0 and (row_hi - row_lo) % ta == 0
    q_off = row_lo // ta
    return pl.pallas_call(
        functools.partial(_attn_kernel, tq=ta, tk=tk, q_off=q_off),
        grid=(B, HP, (row_hi - row_lo) // ta),
        in_specs=[
            pl.BlockSpec((None, 2, ta, LANES), lambda b, p, i: (b, p, i + q_off, 0)),
            pl.BlockSpec((None, 2, S, LANES), lambda b, p, i: (b, p, 0, 0)),
            pl.BlockSpec((None, None, S // tk, LANES, tk), lambda b, p, i: (b, p, 0, 0, 0)),
        ],
        out_specs=pl.BlockSpec((None, ta, LANES), lambda b, p, i: (b, i, p)),
        out_shape=jax.ShapeDtypeStruct((B, row_hi - row_lo, ATTN_WIDTH), BF16),
        scratch_shapes=[pltpu.VMEM((2, LANES, ta), F32), pltpu.VMEM((2, 1, ta), F32)],
        compiler_params=_cparams(("parallel", "parallel", "arbitrary")),
        name="moba_attn",
    )(qaug, kaug, vt)


def _silu(x):
    return x * (1.0 / (1.0 + jnp.exp(-x)))


def _conv_silu(u_ref, halo_ref, w_ref, b_ref, first):
    u = u_ref[...].astype(F32)
    halo = jnp.where(first, 0.0, halo_ref[...].astype(F32))
    ext = jnp.concatenate([halo, u], axis=0)
    w = w_ref[...]
    out = b_ref[...] + w[3:4, :] * u
    for back in range(1, SSD_CONV):
        out = out + w[3 - back:4 - back, :] * pltpu.roll(ext, back, axis=0)[SUBLANES:, :]
    return _silu(out)


def _ssd_group(xs, bm, cm, z, dt, rg, a, d, ng, state_ref):
    L, W = xs.shape
    hp = lax.Precision.HIGHEST
    dtx = jnp.dot(dt, rg, precision=hp, preferred_element_type=F32)
    ax = dtx * a
    rr = lax.broadcasted_iota(I32, (L, L), 0)
    cc = lax.broadcasted_iota(I32, (L, L), 1)
    causal = cc <= rr
    acum = jnp.dot(causal.astype(F32), ax, precision=hp, preferred_element_type=F32)
    acum_t = acum.T
    a_last = acum[L - 1:L, :]
    xdt = xs * dtx
    cmb = cm.astype(BF16)
    cb = lax.dot_general(cmb, bm.astype(BF16), NT_DIMS, preferred_element_type=F32)
    lane = lax.broadcasted_iota(I32, (L, W), 1)
    y = jnp.zeros((L, W), F32)
    for r in range(W // 64):
        seg = acum[:, 64 * r:64 * r + 1] - acum_t[64 * r:64 * r + 1, :]
        lmat = jnp.exp(jnp.where(causal, seg, -jnp.inf))
        xr = jnp.where((lane >= 64 * r) & (lane < 64 * r + 64), xdt, 0.0).astype(BF16)
        y = y + jnp.dot((cb * lmat).astype(BF16), xr, preferred_element_type=F32)
    state = state_ref[...]
    y = y + jnp.dot(cmb, state.astype(BF16), preferred_element_type=F32) * jnp.exp(acum)
    wgt = (xdt * jnp.exp(a_last - acum)).astype(BF16)
    state_ref[...] = state * jnp.exp(a_last) + jnp.dot(bm.T.astype(BF16), wgt, preferred_element_type=F32)
    y = y + d * xs
    y = y * _silu(z)
    y = y * lax.rsqrt(jnp.mean(y * y, axis=-1, keepdims=True) + EPS)
    return y * ng


def _ssd_kernel(x_ref, xh_ref, b_ref, bh_ref, c_ref, ch_ref, z_ref, dt_ref,
                wx_ref, wb_ref, wc_ref, bx_ref, bb_ref, bc_ref, dtb_ref, rg_ref,
                a_ref, d_ref, ng_ref, y_ref, state_ref):
    first = pl.program_id(2) == 0
    W, N = SSD_GROUP_W, SSD_STATE

    @pl.when(first)
    def _():
        state_ref[...] = jnp.zeros_like(state_ref)

    xs = _conv_silu(x_ref, xh_ref, wx_ref, bx_ref, first)
    bm = _conv_silu(b_ref, bh_ref, wb_ref, bb_ref, first)
    cm = _conv_silu(c_ref, ch_ref, wc_ref, bc_ref, first)
    dt = jax.nn.softplus(dt_ref[...] + dtb_ref[...])
    z = z_ref[...].astype(F32)
    for gi in range(SSD_GROUPS_PER_STEP):
        cw, cn = slice(gi * W, (gi + 1) * W), slice(gi * N, (gi + 1) * N)
        y = _ssd_group(xs[:, cw], bm[:, cn], cm[:, cn], z[:, cw], dt, rg_ref[gi],
                       a_ref[:, cw], d_ref[:, cw], ng_ref[:, cw], state_ref.at[gi])
        y_ref[:, cw] = y.astype(y_ref.dtype)


def ssd_mixer(proj3, dtraw3, conv_w, conv_b, dt_bias, a_log, d_skip, norm_g):
    B, S, _ = proj3.shape
    GP = SSD_GROUPS_PER_STEP
    L, W, N = SSD_CHUNK, SSD_GROUP_W * GP, SSD_STATE * GP
    assert S % L == 0 and SSD_GROUPS % GP == 0
    nc = S // L
    hb = L // SUBLANES
    xb, bb, cb_, zb = COL_X // W, COL_B // N, COL_C // N, COL_Z // W
    rep = SSD_INNER // SSD_HEADS
    a_exp = jnp.repeat(-jnp.exp(a_log.astype(F32)), rep)[None, :]
    d_exp = jnp.repeat(d_skip.astype(F32), rep)[None, :]
    ng = norm_g.astype(F32)[None, :]
    dtb = jnp.zeros((1, LANES), F32).at[0, :SSD_HEADS].set(dt_bias.astype(F32))
    head_of_chan = np.arange(SSD_INNER) // rep
    rg = (np.arange(LANES)[None, :, None]
          == head_of_chan.reshape(SSD_GROUPS, 1, SSD_GROUP_W)).astype(np.float32)
    cw = conv_w.astype(F32)
    cbias = conv_b.astype(F32)[None, :]
    cxo, cbo, cco = 0, SSD_INNER // N, (SSD_INNER + SSD_GN) // N

    def halo(col):
        return lambda b, g, c: (b, jnp.maximum(c * hb - 1, 0), col + g)

    return pl.pallas_call(
        _ssd_kernel,
        grid=(B, SSD_GROUPS // GP, nc),
        in_specs=[
            pl.BlockSpec((None, L, W), lambda b, g, c: (b, c, xb + g)),
            pl.BlockSpec((None, SUBLANES, W), halo(xb)),
            pl.BlockSpec((None, L, N), lambda b, g, c: (b, c, bb + g)),
            pl.BlockSpec((None, SUBLANES, N), halo(bb)),
            pl.BlockSpec((None, L, N), lambda b, g, c: (b, c, cb_ + g)),
            pl.BlockSpec((None, SUBLANES, N), halo(cb_)),
            pl.BlockSpec((None, L, W), lambda b, g, c: (b, c, zb + g)),
            pl.BlockSpec((None, L, LANES), lambda b, g, c: (b, c, 0)),
            pl.BlockSpec((SSD_CONV, W), lambda b, g, c: (0, cxo + g)),
            pl.BlockSpec((SSD_CONV, N), lambda b, g, c: (0, cbo + g)),
            pl.BlockSpec((SSD_CONV, N), lambda b, g, c: (0, cco + g)),
            pl.BlockSpec((1, W), lambda b, g, c: (0, cxo + g)),
            pl.BlockSpec((1, N), lambda b, g, c: (0, cbo + g)),
            pl.BlockSpec((1, N), lambda b, g, c: (0, cco + g)),
            pl.BlockSpec((1, LANES), lambda b, g, c: (0, 0)),
            pl.BlockSpec((GP, LANES, SSD_GROUP_W), lambda b, g, c: (g, 0, 0)),
            pl.BlockSpec((1, W), lambda b, g, c: (0, g)),
            pl.BlockSpec((1, W), lambda b, g, c: (0, g)),
            pl.BlockSpec((1, W), lambda b, g, c: (0, g)),
        ],
        out_specs=pl.BlockSpec((None, L, W), lambda b, g, c: (b, c, g)),
        out_shape=jax.ShapeDtypeStruct((B, S, SSD_INNER), BF16),
        scratch_shapes=[pltpu.VMEM((GP, SSD_STATE, SSD_GROUP_W), F32)],
        compiler_params=_cparams(("parallel", "parallel", "arbitrary")),
        name="ssd_scan",
    )(proj3, proj3, proj3, proj3, proj3, proj3, proj3, dtraw3,
      cw, cw, cw, cbias, cbias, cbias, dtb, jnp.asarray(rg), a_exp, d_exp, ng)


def _merge_kernel(x_ref, a_ref, s_ref, ga_ref, gs_ref, wa_ref, ws_ref, wo_ref, o_ref):
    ya = jnp.dot(a_ref[...], wa_ref[...], preferred_element_type=F32)
    ys = jnp.dot(s_ref[...], ws_ref[...], preferred_element_type=F32)
    mixed = jax.nn.sigmoid(ga_ref[...].astype(F32)) * ya + jax.nn.sigmoid(gs_ref[...].astype(F32)) * ys
    o_ref[...] = x_ref[...] + jnp.dot(mixed.astype(BF16), wo_ref[...], preferred_element_type=F32)


def merge_branches(x2, attn2, ssd2, proj2, wa, ws, wo, tm, row_lo):
    T, D = attn2.shape[0], x2.shape[1]
    off = row_lo // tm
    assert row_lo % tm == 0
    full = lambda a: pl.BlockSpec(a.shape, lambda i: (0, 0))
    return pl.pallas_call(
        _merge_kernel,
        grid=(T // tm,),
        in_specs=[
            pl.BlockSpec((tm, D), lambda i: (i + off, 0)),
            pl.BlockSpec((tm, ATTN_WIDTH), lambda i: (i, 0)),
            pl.BlockSpec((tm, SSD_INNER), lambda i: (i + off, 0)),
            pl.BlockSpec((tm, D), lambda i: (i + off, COL_GA // D)),
            pl.BlockSpec((tm, D), lambda i: (i + off, COL_GS // D)),
            full(wa), full(ws), full(wo),
        ],
        out_specs=pl.BlockSpec((tm, D), lambda i: (i, 0)),
        out_shape=jax.ShapeDtypeStruct((T, D), F32),
        compiler_params=_cparams(("parallel",)),
        name="merge",
    )(x2, attn2, ssd2, proj2, proj2, wa, ws, wo)


def _topk_rows(s, idx, k):
    n = s.shape[0]
    row = lax.broadcasted_iota(I32, s.shape, 0)
    vals, rows, picked = [], [], []
    for _ in range(k):
        m = jnp.max(s, axis=0, keepdims=True)
        first = jnp.min(jnp.where(s == m, row, n), axis=0, keepdims=True)
        hit = row == first
        vals.append(m)
        rows.append(first)
        if idx is not None:
            picked.append(jnp.max(jnp.where(hit, idx, -1), axis=0, keepdims=True))
        s = jnp.where(hit, -jnp.inf, s)
    cat = lambda xs: jnp.concatenate(xs, axis=0)
    return cat(vals), cat(rows), (cat(picked) if idx is not None else None)


def _peer_topk_kernel(q_ref, k1_ref, k2_ref, eidx_ref, gw_ref, *, tt):
    half = PEER_QDIM // 2
    e_rows, g_rows = [], []
    for h in range(PEER_HEADS):
        qa = q_ref[:, h * PEER_QDIM:h * PEER_QDIM + half].astype(BF16)
        qb = q_ref[:, h * PEER_QDIM + half:(h + 1) * PEER_QDIM].astype(BF16)
        s1 = lax.dot_general(k1_ref[h], qa, NT_DIMS, preferred_element_type=F32)
        s2 = lax.dot_general(k2_ref[h], qb, NT_DIMS, preferred_element_type=F32)
        v1, i1, _ = _topk_rows(s1, None, PEER_TOPK)
        v2, i2, _ = _topk_rows(s2, None, PEER_TOPK)
        sub = lax.broadcasted_iota(I32, (SUBLANES, tt), 0)
        cand_parts = [v1[0:1, :] + v2]
        cidx_parts = [i1[0:1, :] * PEER_NKEYS + i2]
        for a in range(1, SUBLANES):
            keep = sub < PEER_TOPK // (a + 1)
            cand_parts.append(jnp.where(keep, v1[a:a + 1, :] + v2[0:SUBLANES, :], -jnp.inf))
            cidx_parts.append(i1[a:a + 1, :] * PEER_NKEYS + i2[0:SUBLANES, :])
        cand_parts.append(v1[SUBLANES:, :] + v2[0:1, :])
        cidx_parts.append(i1[SUBLANES:, :] * PEER_NKEYS + i2[0:1, :])
        cand = jnp.concatenate(cand_parts, axis=0)
        cidx = jnp.concatenate(cidx_parts, axis=0)
        sv, _, ex = _topk_rows(cand, cidx, PEER_TOPK)
        e = jnp.exp(sv - sv[0:1, :])
        g_rows.append(e / jnp.sum(e, axis=0, keepdims=True))
        e_rows.append(ex)
    eidx_ref[...] = jnp.concatenate(e_rows, axis=0).T
    gw_ref[...] = jnp.concatenate(g_rows, axis=0).T


def peer_topk(q, keys1, keys2, tt):
    T = q.shape[0]
    full3 = lambda a: pl.BlockSpec(a.shape, lambda i: (0, 0, 0))
    return pl.pallas_call(
        functools.partial(_peer_topk_kernel, tt=tt),
        grid=(T // tt,),
        in_specs=[pl.BlockSpec((tt, q.shape[1]), lambda i: (i, 0)), full3(keys1), full3(keys2)],
        out_specs=[
            pl.BlockSpec((tt, PEER_SLOTS), lambda i: (i, 0)),
            pl.BlockSpec((tt, PEER_SLOTS), lambda i: (i, 0)),
        ],
        out_shape=[
            jax.ShapeDtypeStruct((T, PEER_SLOTS), I32),
            jax.ShapeDtypeStruct((T, PEER_SLOTS), F32),
        ],
        compiler_params=_cparams(("parallel",)),
        name="peer_topk",
    )(q, keys1, keys2)


def pack_table(tab):
    half = tab.shape[1] // 2
    bits = lax.bitcast_convert_type(tab.astype(BF16), jnp.uint16).astype(jnp.uint32)
    return lax.bitcast_convert_type((bits[:, :half] << 16) | bits[:, half:], I32)


def sc_scores_and_rows(u_tab, v_tab, idx, xn, win=32, tok_chunk=8):
    W = v_tab.shape[1]
    D = xn.shape[1]
    N = idx.shape[0]
    T = xn.shape[0]
    slots = N // T
    info = plsc.get_sparse_core_info()
    n_cores, n_sub, L = info.num_cores, info.num_subcores, info.num_lanes
    workers = n_cores * n_sub
    tok_w = T // workers
    per_tok = slots // win
    csteps = tok_chunk * per_tok
    assert tok_w * workers == T and per_tok * win == slots and per_tok % 2 == 0 and tok_w % tok_chunk == 0
    assert u_tab.shape[1] == W and D == 2 * W and W % L == 0 and win % L == 0
    mesh = plsc.VectorSubcoreMesh(core_axis_name="c", subcore_axis_name="s")
    dma = pltpu.SemaphoreType.DMA

    @functools.partial(
        pl.kernel, mesh=mesh,
        out_type=[jax.ShapeDtypeStruct((N,), F32), jax.ShapeDtypeStruct((N, W), v_tab.dtype)],
        scratch_types=[pltpu.VMEM((csteps, win), I32),
                       pltpu.VMEM((win, W), I32), pltpu.VMEM((win, W), I32),
                       pltpu.VMEM((win, W), v_tab.dtype), pltpu.VMEM((win, W), v_tab.dtype),
                       pltpu.VMEM((D,), F32), pltpu.VMEM((slots,), F32),
                       dma, dma, dma, dma, dma, dma],
        compiler_params=pltpu.CompilerParams(needs_layout_passes=False),
    )
    def k(u_hbm, v_hbm, idx_hbm, xn_hbm, act_hbm, rows_hbm,
          idx_v, u0, u1, v0, v1, x_v, act_v, gu0, gu1, gv0, gv1, wv0, wv1):
        wid = lax.axis_index("s") * n_cores + lax.axis_index("c")
        tok0 = wid * tok_w
        ubuf, vbuf = (u0, u1), (v0, v1)
        gu, gv, wv = (gu0, gu1), (gv0, gv1), (wv0, wv1)
        lane = lax.iota(I32, L)

        def u_gather(s, slot):
            return pltpu.make_async_copy(u_hbm.at[idx_v.at[s]], ubuf[slot], gu[slot])

        def v_gather(s, slot):
            return pltpu.make_async_copy(v_hbm.at[idx_v.at[s]], vbuf[slot], gv[slot])

        def v_write(gstep, slot):
            off = pl.multiple_of(gstep * win, win)
            return pltpu.make_async_copy(vbuf[slot], rows_hbm.at[pl.ds(off, win)], wv[slot])

        def compute(rows, base):
            def body(kk, accs):
                k16 = pl.multiple_of(kk * L, L)
                xh = x_v[pl.ds(k16, L)]
                xl = x_v[pl.ds(W + k16, L)]
                out = []
                for r in range(win):
                    w = rows[r, pl.ds(k16, L)]
                    hi = lax.bitcast_convert_type(w & jnp.int32(-65536), F32)
                    lo = lax.bitcast_convert_type(w << 16, F32)
                    out.append(accs[r] + hi * xh + lo * xl)
                return tuple(out)

            accs = lax.fori_loop(0, W // L, body, tuple(jnp.zeros((L,), F32) for _ in range(win)))
            for j in range(win // L):
                res = jnp.zeros((L,), F32)
                for r in range(L):
                    res = jnp.where(lane == r, jnp.sum(accs[j * L + r]), res)
                act_v[pl.ds(base + j * L, L)] = res

        @pl.loop(0, tok_w // tok_chunk)
        def _(c):
            tokc = tok0 + c * tok_chunk
            gbase = tokc * per_tok
            pltpu.sync_copy(idx_hbm.at[pl.ds(pl.multiple_of(gbase, SUBLANES), csteps)], idx_v)
            u_gather(0, 0).start()
            v_gather(0, 0).start()

            @pl.loop(0, tok_chunk)
            def _(tt):
                pltpu.sync_copy(xn_hbm.at[tokc + tt], x_v)
                for h in range(per_tok):
                    s = tt * per_tok + h
                    slot, nslot = h % 2, (h + 1) % 2

                    @pl.when(s + 1 < csteps)
                    def _():
                        u_gather(s + 1, nslot).start()

                        @pl.when(s >= 1)
                        def _():
                            v_write(gbase + s - 1, nslot).wait()

                        v_gather(s + 1, nslot).start()

                    v_gather(s, slot).wait()
                    v_write(gbase + s, slot).start()
                    u_gather(s, slot).wait()
                    compute(ubuf[slot], h * win)
                pltpu.sync_copy(act_v, act_hbm.at[pl.ds(pl.multiple_of((tokc + tt) * slots, slots), slots)])

            v_write(gbase + csteps - 2, 0).wait()
            v_write(gbase + csteps - 1, 1).wait()

    return k(u_tab, v_tab, idx.reshape(N // win, win), xn)


def _unpack_words(w):
    u = pltpu.bitcast(w, jnp.uint32)
    hi = pltpu.bitcast(u & jnp.uint32(0xFFFF0000), F32)
    lo = pltpu.bitcast(u << 16, F32)
    return hi, lo


def _rms_scale_kernel(x_ref, g_ref, o_ref):
    x = x_ref[...]
    o_ref[...] = x * lax.rsqrt(jnp.mean(x * x, axis=-1, keepdims=True) + EPS) * g_ref[...]


def rms_scale(x, g, tm):
    T, D = x.shape
    return pl.pallas_call(
        _rms_scale_kernel,
        grid=(T // tm,),
        in_specs=[pl.BlockSpec((tm, D), lambda i: (i, 0)), pl.BlockSpec((1, D), lambda i: (0, 0))],
        out_specs=pl.BlockSpec((tm, D), lambda i: (i, 0)),
        out_shape=jax.ShapeDtypeStruct((T, D), F32),
        compiler_params=_cparams(("parallel",)),
        name="rms_scale",
    )(x, g)


def _peer_expert_kernel(x_ref, act_ref, gw_ref, vg_ref, o_ref, *, tt):
    half = D_MODEL // 2
    x1 = x_ref[...]
    act = act_ref[...]
    gelu = 0.5 * act * (1.0 + lax.erf(act * (2.0 ** -0.5)))
    hact = gelu * gw_ref[...]
    hact_t = jnp.concatenate([hact, jnp.zeros((LANES - tt, PEER_SLOTS), F32)], axis=0).T
    lane = lax.broadcasted_iota(I32, (PEER_SLOTS, LANES), 1)
    for t in range(tt):
        hcol = jnp.sum(jnp.where(lane == t, hact_t, 0.0), axis=-1, keepdims=True)
        hi, lo = _unpack_words(vg_ref[t * PEER_SLOTS:(t + 1) * PEER_SLOTS, :])
        o_ref[t:t + 1, :half] = x1[t:t + 1, :half] + jnp.sum(hcol * hi, axis=0, keepdims=True)
        o_ref[t:t + 1, half:] = x1[t:t + 1, half:] + jnp.sum(hcol * lo, axis=0, keepdims=True)


def peer_experts(x1, act, gw, vg, tt=PEER_TOKENS_PER_STEP):
    T, D = x1.shape
    W = vg.shape[1]
    return pl.pallas_call(
        functools.partial(_peer_expert_kernel, tt=tt),
        grid=(T // tt,),
        in_specs=[
            pl.BlockSpec((tt, D), lambda i: (i, 0)),
            pl.BlockSpec((tt, PEER_SLOTS), lambda i: (i, 0)),
            pl.BlockSpec((tt, PEER_SLOTS), lambda i: (i, 0)),
            pl.BlockSpec((tt * PEER_SLOTS, W), lambda i: (i, 0)),
        ],
        out_specs=pl.BlockSpec((tt, D), lambda i: (i, 0)),
        out_shape=jax.ShapeDtypeStruct((T, D), F32),
        compiler_params=_cparams(("parallel",)),
        name="peer_experts",
    )(x1, act, gw, vg)


def peer_ffn_residual(x1, norm2_g, wq, keys1, keys2, u_tab, v_packed):
    T = x1.shape[0]
    g2 = norm2_g.astype(F32)[None, :]
    q = norm_matmul(x1, g2, wq, F32, min(1024, T), 1024)
    eidx, gw = peer_topk(q, keys1, keys2, min(256, T))
    flat = eidx.reshape(-1)
    xn = rms_scale(x1, g2, min(512, T))
    act, vg = sc_scores_and_rows(u_tab, v_packed, flat, xn)
    return peer_experts(x1, act.reshape(T, PEER_SLOTS), gw, vg)


def kernel(x, norm1_g, w_in, q_norm_g, k_norm_g, conv_w, conv_b, dt_bias, a_log, d_skip, ssd_norm_g,
           w_attn_o, w_ssd_o, w_out, norm2_g, w_peer_q, peer_keys1, peer_keys2, peer_u, peer_v):
    B, S, D = x.shape
    xs = [x[b] for b in range(B)]
    for l in range(norm1_g.shape[0]):
        w = w_in[l]
        dt0 = COL_GA
        w_main = jnp.concatenate([w[:, :dt0], w[:, dt0 + SSD_HEADS:]], axis=1).astype(BF16)
        w_dt = jnp.zeros((D, LANES), BF16).at[:, :SSD_HEADS].set(w[:, dt0:dt0 + SSD_HEADS].astype(BF16))
        g1 = norm1_g[l].astype(F32)[None, :]
        wa, ws, wo = w_attn_o[l].astype(BF16), w_ssd_o[l].astype(BF16), w_out[l].astype(BF16)
        wq = w_peer_q[l].astype(BF16)
        k1, k2 = peer_keys1[l].astype(BF16), peer_keys2[l].astype(BF16)
        up, vp = pack_table(peer_u[l]), pack_table(peer_v[l])
        tm = min(1024, S)
        rows = S // min(PEER_CHUNKS, S // MOBA_Q_TILE) if S >= MOBA_Q_TILE else S
        for b in range(B):
            x2 = xs[b]
            proj = norm_matmul(x2, g1, w_main, BF16, tm, 1024)
            dtraw = norm_matmul(x2, g1, w_dt, F32, tm, LANES)
            proj3 = proj[None]
            yssd = ssd_mixer(proj3, dtraw[None], conv_w[l], conv_b[l], dt_bias[l],
                             a_log[l], d_skip[l], ssd_norm_g[l])
            qaug, kaug, vt = moba_prepare(proj3, q_norm_g[l], k_norm_g[l])
            outs = []
            for lo in range(0, S, rows):
                attn = moba_attention(qaug, kaug, vt, lo, lo + rows)
                x1 = merge_branches(x2, attn[0], yssd[0], proj, wa, ws, wo, min(512, rows), lo)
                outs.append(peer_ffn_residual(x1, norm2_g[l], wq, k1, k2, up, vp))
            xs[b] = jnp.concatenate(outs, axis=0)
    return jnp.stack(xs, axis=0)
```
